```python
import math
import jax, jax.numpy as jnp
from jax import lax
import numpy as np

D_MODEL = 1024
BATCH = 4
SEQ = 8192
DEPTH = 1

CHUNK = 64
Q_BLOCK = 128
EPS = 1e-6
CONV_CH = D_MODEL // 2
CONV_WIDTH = 31
MLA_HEADS = 8
MLA_NOPE = 64
MLA_ROPE = 32
MLA_V = 64
MLA_Q_RANK = 256
MLA_KV_RANK = 128
ROPE_THETA = 10000.0
MIX_WIDTH = CONV_CH + MLA_HEADS * MLA_V
IN_COLS = 2 * CONV_CH + MLA_Q_RANK + MLA_KV_RANK + MLA_ROPE
MEM_LEN = 256
MEM_HEADS = 4
MEM_HEAD_DIM = D_MODEL // MEM_HEADS
D_FF = 2816
FFN_CONV_WIDTH = 3
MAX_START = 4096

kernel_name = "hybrid_conformer_mla_stream_layer"


def rms_norm(x, g):
    xf = x.astype(jnp.float32)
    y = xf * lax.rsqrt(jnp.mean(xf * xf, axis=-1, keepdims=True) + EPS)
    return (y * g.astype(jnp.float32)).astype(x.dtype)


def layer_norm(x, g, b):
    xf = x.astype(jnp.float32)
    mu = jnp.mean(xf, axis=-1, keepdims=True)
    xc = xf - mu
    y = xc * lax.rsqrt(jnp.mean(xc * xc, axis=-1, keepdims=True) + EPS)
    return (y * g.astype(jnp.float32) + b.astype(jnp.float32)).astype(x.dtype)


def causal_depthwise_conv(x, w, b):
    k_width, ch = w.shape
    y = lax.conv_general_dilated(
        x, w[:, None, :].astype(x.dtype), window_strides=(1,), padding=[(k_width - 1, 0)],
        dimension_numbers=('NWC', 'WIO', 'NWC'), feature_group_count=ch)
    return y + b.astype(x.dtype)


def rope_tables(positions, dim):
    inv_freq = ROPE_THETA ** (-jnp.arange(0, dim, 2, dtype=jnp.float32) / dim)
    ang = positions.astype(jnp.float32)[..., None] * inv_freq
    return jnp.cos(ang), jnp.sin(ang)


def apply_rope(x, cos, sin):
    x1, x2 = jnp.split(x.astype(jnp.float32), 2, axis=-1)
    return jnp.concatenate([x1 * cos - x2 * sin, x1 * sin + x2 * cos], axis=-1).astype(x.dtype)


def chunk_causal_attention(q, k, v, scale):
    bsz, seq, heads, dk = q.shape
    n_blocks = seq // Q_BLOCK
    q_blocks = q.reshape(bsz, n_blocks, Q_BLOCK, heads, dk).transpose(1, 0, 2, 3, 4)
    key_chunk = jnp.arange(seq) // CHUNK

    def one_block(args):
        i, qi = args
        s = jnp.einsum('bqhd,bkhd->bhqk', qi, k, preferred_element_type=jnp.float32) * scale
        q_chunk = (i * Q_BLOCK + jnp.arange(Q_BLOCK)) // CHUNK
        mask = key_chunk[None, :] <= q_chunk[:, None]
        s = jnp.where(mask[None, None], s, -jnp.inf)
        p = jax.nn.softmax(s, axis=-1)
        return jnp.einsum('bhqk,bkhd->bqhd', p.astype(v.dtype), v)

    out = lax.map(one_block, (jnp.arange(n_blocks), q_blocks))
    return out.transpose(1, 0, 2, 3, 4).reshape(bsz, seq, heads, v.shape[-1])


def hybrid_mixer(h, cos, sin, w_in, b_conv_in, w_conv_dw, b_conv_dw, conv_ln_g, conv_ln_b,
                 q_lat_norm_g, w_uq, kv_lat_norm_g, w_ukv, q_norm_g, k_norm_g, w_out):
    bsz, seq, _ = h.shape
    z = h @ w_in
    s1 = 2 * CONV_CH
    s2 = s1 + MLA_Q_RANK
    s3 = s2 + MLA_KV_RANK
    conv_in, c_q, c_kv, k_rope = jnp.split(z, [s1, s2, s3], axis=-1)

    a, gate = jnp.split(conv_in + b_conv_in, 2, axis=-1)
    u = a * jax.nn.sigmoid(gate)
    u = causal_depthwise_conv(u, w_conv_dw, b_conv_dw)
    u = jax.nn.silu(layer_norm(u, conv_ln_g, conv_ln_b))

    q = (rms_norm(c_q, q_lat_norm_g) @ w_uq).reshape(bsz, seq, MLA_HEADS, MLA_NOPE + MLA_ROPE)
    kv = (rms_norm(c_kv, kv_lat_norm_g) @ w_ukv).reshape(bsz, seq, MLA_HEADS, MLA_NOPE + MLA_V)
    k_nope, v = jnp.split(kv, [MLA_NOPE], axis=-1)
    k_r = jnp.broadcast_to(k_rope[:, :, None, :], (bsz, seq, MLA_HEADS, MLA_ROPE))
    k = jnp.concatenate([k_nope, k_r], axis=-1)
    q = rms_norm(q, q_norm_g)
    k = rms_norm(k, k_norm_g)
    q = jnp.concatenate([q[..., :MLA_NOPE], apply_rope(q[..., MLA_NOPE:], cos, sin)], axis=-1)
    k = jnp.concatenate([k[..., :MLA_NOPE], apply_rope(k[..., MLA_NOPE:], cos, sin)], axis=-1)
    attn = chunk_causal_attention(q, k, v, 1.0 / math.sqrt(MLA_NOPE + MLA_ROPE))
    attn = attn.reshape(bsz, seq, MLA_HEADS * MLA_V)

    return jnp.concatenate([u, attn], axis=-1) @ w_out


def memory_cross_attention(hq, hm, w_mem_q, w_mem_kv, mem_q_norm_g, mem_k_norm_g, w_mem_o):
    bsz, seq, _ = hq.shape
    q = (hq @ w_mem_q).reshape(bsz, seq, MEM_HEADS, MEM_HEAD_DIM)
    k, v = jnp.split(hm @ w_mem_kv, 2, axis=-1)
    k = k.reshape(bsz, MEM_LEN, MEM_HEADS, MEM_HEAD_DIM)
    v = v.reshape(bsz, MEM_LEN, MEM_HEADS, MEM_HEAD_DIM)
    q = rms_norm(q, mem_q_norm_g)
    k = rms_norm(k, mem_k_norm_g)
    s = jnp.einsum('bqhd,bkhd->bhqk', q, k, preferred_element_type=jnp.float32) / math.sqrt(MEM_HEAD_DIM)
    p = jax.nn.softmax(s, axis=-1)
    o = jnp.einsum('bhqk,bkhd->bqhd', p.astype(v.dtype), v).reshape(bsz, seq, D_MODEL)
    return o @ w_mem_o


def conv_gated_ffn(h, w_up, w_ffn_dw, b_ffn_dw, w_down):
    up = causal_depthwise_conv(h @ w_up, w_ffn_dw, b_ffn_dw)
    g, val = jnp.split(up, 2, axis=-1)
    return (jax.nn.silu(g) * val) @ w_down


def setup_inputs(seed: int = 0) -> dict:
    key = jax.random.key(seed)
    ks = iter(jax.random.split(key, 40))
    L = DEPTH

    def w(shape, fan_in):
        return jax.random.normal(next(ks), shape, jnp.float32) * fan_in ** -0.5

    def gain(shape):
        return 1.0 + 0.05 * jax.random.normal(next(ks), shape, jnp.float32)

    def bias(shape):
        return 0.02 * jax.random.normal(next(ks), shape, jnp.float32)

    x = jax.random.normal(next(ks), (BATCH, SEQ, D_MODEL), jnp.float32)
    mem = jax.random.normal(next(ks), (BATCH, MEM_LEN, D_MODEL), jnp.float32)
    start = jax.random.randint(next(ks), (BATCH, 1), 0, MAX_START, dtype=jnp.int32)
    positions = start + jnp.arange(SEQ, dtype=jnp.int32)[None, :]
    return {
        "x": x,
        "mem": mem,
        "positions": positions,
        "mix_norm_g": gain((L, D_MODEL)),
        "w_in": w((L, D_MODEL, IN_COLS), D_MODEL),
        "b_conv_in": bias((L, 2 * CONV_CH)),
        "w_conv_dw": w((L, CONV_WIDTH, CONV_CH), CONV_WIDTH),
        "b_conv_dw": bias((L, CONV_CH)),
        "conv_ln_g": gain((L, CONV_CH)),
        "conv_ln_b": bias((L, CONV_CH)),
        "q_lat_norm_g": gain((L, MLA_Q_RANK)),
        "w_uq": w((L, MLA_Q_RANK, MLA_HEADS * (MLA_NOPE + MLA_ROPE)), MLA_Q_RANK),
        "kv_lat_norm_g": gain((L, MLA_KV_RANK)),
        "w_ukv": w((L, MLA_KV_RANK, MLA_HEADS * (MLA_NOPE + MLA_V)), MLA_KV_RANK),
        "q_norm_g": gain((L, MLA_NOPE + MLA_ROPE)),
        "k_norm_g": gain((L, MLA_NOPE + MLA_ROPE)),
        "w_out": w((L, MIX_WIDTH, D_MODEL), MIX_WIDTH),
        "mem_norm_x_g": gain((L, D_MODEL)),
        "mem_norm_m_g": gain((L, D_MODEL)),
        "w_mem_q": w((L, D_MODEL, D_MODEL), D_MODEL),
        "w_mem_kv": w((L, D_MODEL, 2 * D_MODEL), D_MODEL),
        "mem_q_norm_g": gain((L, MEM_HEAD_DIM)),
        "mem_k_norm_g": gain((L, MEM_HEAD_DIM)),
        "w_mem_o": w((L, D_MODEL, D_MODEL), D_MODEL),
        "ffn_norm_g": gain((L, D_MODEL)),
        "w_up": w((L, D_MODEL, 2 * D_FF), D_MODEL),
        "w_ffn_dw": w((L, FFN_CONV_WIDTH, 2 * D_FF), FFN_CONV_WIDTH),
        "b_ffn_dw": bias((L, 2 * D_FF)),
        "w_down": w((L, D_FF, D_MODEL), D_FF),
    }


def reference(x, mem, positions, mix_norm_g, w_in, b_conv_in, w_conv_dw, b_conv_dw, conv_ln_g,
              conv_ln_b, q_lat_norm_g, w_uq, kv_lat_norm_g, w_ukv, q_norm_g, k_norm_g, w_out,
              mem_norm_x_g, mem_norm_m_g, w_mem_q, w_mem_kv, mem_q_norm_g, mem_k_norm_g, w_mem_o,
              ffn_norm_g, w_up, w_ffn_dw, b_ffn_dw, w_down):
    cos, sin = rope_tables(positions, MLA_ROPE)
    cos, sin = cos[:, :, None, :], sin[:, :, None, :]
    for l in range(DEPTH):
        h = rms_norm(x, mix_norm_g[l])
        x = x + hybrid_mixer(h, cos, sin, w_in[l], b_conv_in[l], w_conv_dw[l], b_conv_dw[l],
                             conv_ln_g[l], conv_ln_b[l], q_lat_norm_g[l], w_uq[l],
                             kv_lat_norm_g[l], w_ukv[l], q_norm_g[l], k_norm_g[l], w_out[l])
        hq = rms_norm(x, mem_norm_x_g[l])
        hm = rms_norm(mem, mem_norm_m_g[l])
        x = x + memory_cross_attention(hq, hm, w_mem_q[l], w_mem_kv[l], mem_q_norm_g[l],
                                       mem_k_norm_g[l], w_mem_o[l])
        h = rms_norm(x, ffn_norm_g[l])
        x = x + conv_gated_ffn(h, w_up[l], w_ffn_dw[l], b_ffn_dw[l], w_down[l])
    return x
```

```python
import functools
import math

import jax
import jax.numpy as jnp
from jax import lax
from jax.experimental import pallas as pl
from jax.experimental.pallas import tpu as pltpu

F32 = jnp.float32
BF16 = jnp.bfloat16

D_MODEL = 1024
CHUNK = 64
EPS = 1e-6
CONV_CH = 512
CONV_WIDTH = 31
MLA_HEADS = 8
MLA_NOPE = 64
MLA_ROPE = 32
MLA_QK = MLA_NOPE + MLA_ROPE
MLA_V = 64
MLA_Q_RANK = 256
MLA_KV_RANK = 128
ROPE_THETA = 10000.0
MEM_LEN = 256
MEM_HEADS = 4
MEM_HEAD_DIM = 256
D_FF = 2816
FFN_CONV_WIDTH = 3

LANES = 128
HALO = 32
FFN_HALO = 8
FF_CHUNK = 256
N_FF_CHUNKS = D_FF // FF_CHUNK
LOG2E = 1.4426950408889634

TM_IN = 512
TM_MIX = 512
TM_FFN = 512
TQ = 512
VMEM_LIMIT = 56 * 1024 * 1024


def _rms(x, g):
    return x * lax.rsqrt(jnp.mean(x * x, axis=-1, keepdims=True) + EPS) * g


def _const_spec(shape):
    zeros = (0,) * len(shape)
    return pl.BlockSpec(shape, lambda *_: zeros, pipeline_mode=pl.Buffered(1))


def _memkv_kernel(mem_ref, g_ref, w_ref, kg_ref, k_ref, v_ref):
    hm = _rms(mem_ref[0], g_ref[...]).astype(BF16)
    kv = jnp.dot(hm, w_ref[...], preferred_element_type=F32)
    for h in range(MEM_HEADS):
        sl = slice(h * MEM_HEAD_DIM, (h + 1) * MEM_HEAD_DIM)
        k_ref[0, :, sl] = _rms(kv[:, sl], kg_ref[...]).astype(BF16)
    v_ref[0] = kv[:, D_MODEL:].astype(BF16)


def _memkv(mem, g, w_kv, kg):
    b = mem.shape[0]
    return pl.pallas_call(
        _memkv_kernel,
        grid=(b,),
        in_specs=[
            pl.BlockSpec((1, MEM_LEN, D_MODEL), lambda i: (i, 0, 0)),
            _const_spec((1, D_MODEL)),
            _const_spec((D_MODEL, 2 * D_MODEL)),
            _const_spec((1, MEM_HEAD_DIM)),
        ],
        out_specs=[
            pl.BlockSpec((1, MEM_LEN, D_MODEL), lambda i: (i, 0, 0)),
            pl.BlockSpec((1, MEM_LEN, D_MODEL), lambda i: (i, 0, 0)),
        ],
        out_shape=[jax.ShapeDtypeStruct((b, MEM_LEN, D_MODEL), BF16)] * 2,
        compiler_params=pltpu.CompilerParams(
            dimension_semantics=("arbitrary",), vmem_limit_bytes=VMEM_LIMIT),
        name="memkv",
    )(mem, g, w_kv, kg)


def _inproj_kernel(x_ref, pos_ref, g_ref, wc_ref, bc_ref, ws_ref, gq_ref, wuq_ref, gkv_ref,
                   wuk_ref, wuv_ref, qg_ref, kg_ref, freq_ref, sgn_ref,
                   ug_ref, q_ref, k_ref, v_ref):
    h = _rms(x_ref[0], g_ref[...]).astype(BF16)
    zc = jnp.dot(h, wc_ref[...], preferred_element_type=F32) + bc_ref[...]
    ug_ref[0] = zc[:, :CONV_CH] * jax.nn.sigmoid(zc[:, CONV_CH:])

    zs = jnp.dot(h, ws_ref[...], preferred_element_type=F32)
    cqn = _rms(zs[:, :MLA_Q_RANK], gq_ref[...]).astype(BF16)
    kvn = _rms(zs[:, MLA_Q_RANK:MLA_Q_RANK + MLA_KV_RANK], gkv_ref[...]).astype(BF16)
    k_rope = zs[:, MLA_Q_RANK + MLA_KV_RANK:]
    q_all = jnp.dot(cqn, wuq_ref[...], preferred_element_type=F32)
    k_all = jnp.dot(kvn, wuk_ref[...], preferred_element_type=F32)
    v_ref[0] = jnp.dot(kvn, wuv_ref[...], preferred_element_type=F32).astype(BF16)

    ang = pos_ref[0].astype(F32) * freq_ref[...]
    cos_t = jnp.cos(ang)
    sin_t = jnp.sin(ang) * sgn_ref[...]
    first_half = lax.broadcasted_iota(jnp.int32, (1, LANES), 1) < MLA_NOPE + MLA_ROPE // 2

    def norm_rope(slab, gain):
        r = lax.rsqrt(jnp.sum(slab * slab, axis=-1, keepdims=True) * (1.0 / MLA_QK) + EPS)
        y = slab * r * gain
        partner = jnp.where(first_half,
                            pltpu.roll(y, LANES - MLA_ROPE // 2, 1),
                            pltpu.roll(y, MLA_ROPE // 2, 1))
        return y * cos_t + partner * sin_t

    for hd in range(MLA_HEADS):
        sl = slice(hd * LANES, (hd + 1) * LANES)
        q_ref[0, hd] = norm_rope(q_all[:, sl], qg_ref[...]).astype(BF16)
        k_ref[0, hd] = norm_rope(k_all[:, sl] + k_rope, kg_ref[...]).astype(BF16)


def _inproj(x, pos3, g, wc, bc, ws, gq, wuq, gkv, wuk, wuv, qg, kg, freq, sgn):
    b, s, _ = x.shape
    tm = TM_IN
    tile = lambda w: pl.BlockSpec((1, tm, w), lambda bi, i: (bi, i, 0))
    heads = pl.BlockSpec((1, MLA_HEADS, tm, LANES), lambda bi, i: (bi, 0, i, 0))
    consts = [g, wc, bc, ws, gq, wuq, gkv, wuk, wuv, qg, kg, freq, sgn]
    return pl.pallas_call(
        _inproj_kernel,
        grid=(b, s // tm),
        in_specs=[tile(D_MODEL), tile(1)] + [_const_spec(c.shape) for c in consts],
        out_specs=[tile(CONV_CH), heads, heads, tile(MLA_HEADS * MLA_V)],
        out_shape=[
            jax.ShapeDtypeStruct((b, s, CONV_CH), F32),
            jax.ShapeDtypeStruct((b, MLA_HEADS, s, LANES), BF16),
            jax.ShapeDtypeStruct((b, MLA_HEADS, s, LANES), BF16),
            jax.ShapeDtypeStruct((b, s, MLA_HEADS * MLA_V), BF16),
        ],
        compiler_params=pltpu.CompilerParams(
            dimension_semantics=("arbitrary", "arbitrary"), vmem_limit_bytes=VMEM_LIMIT),
        name="inproj",
    )(x, pos3, *consts)


def _attn_kernel(q_ref, k_ref, v_ref, o_ref, m_ref, l_ref, acc_ref):
    i = pl.program_id(2)
    j = pl.program_id(3)

    @pl.when(j == 0)
    def _init():
        m_ref[...] = jnp.full(m_ref.shape, -jnp.inf, F32)
        l_ref[...] = jnp.zeros(l_ref.shape, F32)
        acc_ref[...] = jnp.zeros(acc_ref.shape, F32)

    def step(masked):
        v = v_ref[0]
        for h in range(2):
            s = lax.dot_general(q_ref[0, h], k_ref[0, h], (((1,), (1,)), ((), ())),
                                preferred_element_type=F32)
            if masked:
                rows = lax.broadcasted_iota(jnp.int32, s.shape, 0) // CHUNK
                cols = lax.broadcasted_iota(jnp.int32, s.shape, 1) // CHUNK
                s = jnp.where(cols <= rows, s, -jnp.inf)
            m_prev = m_ref[h]
            m_new = jnp.maximum(m_prev, jnp.max(s, axis=-1, keepdims=True))
            alpha = jnp.exp2(m_prev - m_new)
            p = jnp.exp2(s - m_new)
            l_ref[h] = alpha * l_ref[h] + jnp.sum(p, axis=-1, keepdims=True)
            acc_ref[h] = alpha * acc_ref[h] + jnp.dot(p.astype(BF16), v, preferred_element_type=F32)
            m_ref[h] = m_new

    @pl.when(j < i)
    def _full():
        step(False)

    @pl.when(j == i)
    def _diag():
        step(True)
        lane = lax.broadcasted_iota(jnp.int32, (1, LANES), 1)
        out = jnp.where(lane < MLA_V, acc_ref[0] / l_ref[0], acc_ref[1] / l_ref[1])
        o_ref[0] = out.astype(BF16)


def _attn(q, k, v):
    b, nh, s, _ = q.shape
    nt = s // TQ
    kv_idx = lambda j, i: jnp.minimum(j, i)
    return pl.pallas_call(
        _attn_kernel,
        grid=(b, nh // 2, nt, nt),
        in_specs=[
            pl.BlockSpec((1, 2, TQ, LANES), lambda bi, hp, i, j: (bi, hp, i, 0)),
            pl.BlockSpec((1, 2, TQ, LANES), lambda bi, hp, i, j: (bi, hp, kv_idx(j, i), 0)),
            pl.BlockSpec((1, TQ, LANES), lambda bi, hp, i, j: (bi, kv_idx(j, i), hp)),
        ],
        out_specs=pl.BlockSpec((1, TQ, LANES), lambda bi, hp, i, j: (bi, i, hp)),
        out_shape=jax.ShapeDtypeStruct((b, s, nh * MLA_V), BF16),
        scratch_shapes=[
            pltpu.VMEM((2, TQ, 1), F32),
            pltpu.VMEM((2, TQ, 1), F32),
            pltpu.VMEM((2, TQ, LANES), F32),
        ],
        compiler_params=pltpu.CompilerParams(
            dimension_semantics=("arbitrary",) * 4, vmem_limit_bytes=VMEM_LIMIT),
        name="attn",
    )(q, k, v)


def _mixmem_kernel(x_ref, ug_ref, halo_ref, attn_ref, kmem_ref, vmem_ref,
                   wdw_ref, bdw_ref, lng_ref, lnb_ref, wou_ref, woa_ref,
                   gx_ref, wq_ref, qg_ref, wo_ref, o_ref, ubuf_ref):
    tm = TM_MIX
    ubuf_ref[0:HALO] = jnp.where(pl.program_id(1) > 0, halo_ref[0], 0.0)
    ubuf_ref[HALO:] = ug_ref[0]
    first = HALO - (CONV_WIDTH - 1)
    y = jnp.zeros((tm, CONV_CH), F32) + bdw_ref[...]
    for t in range(CONV_WIDTH):
        y = y + wdw_ref[t:t + 1, :] * ubuf_ref[pl.ds(first + t, tm), :]
    mu = jnp.mean(y, axis=-1, keepdims=True)
    yc = y - mu
    yn = yc * lax.rsqrt(jnp.mean(yc * yc, axis=-1, keepdims=True) + EPS) * lng_ref[...] + lnb_ref[...]
    u = (yn * jax.nn.sigmoid(yn)).astype(BF16)
    mix = (jnp.dot(u, wou_ref[...], preferred_element_type=F32)
           + jnp.dot(attn_ref[0], woa_ref[...], preferred_element_type=F32))
    x1 = x_ref[0] + mix

    hq = _rms(x1, gx_ref[...]).astype(BF16)
    q = jnp.dot(hq, wq_ref[...], preferred_element_type=F32)
    outs = []
    for h in range(MEM_HEADS):
        sl = slice(h * MEM_HEAD_DIM, (h + 1) * MEM_HEAD_DIM)
        qn = (_rms(q[:, sl], qg_ref[...]) * (LOG2E / math.sqrt(MEM_HEAD_DIM))).astype(BF16)
        s = lax.dot_general(qn, kmem_ref[0, :, sl], (((1,), (1,)), ((), ())),
                            preferred_element_type=F32)
        p = jnp.exp2(s - jnp.max(s, axis=-1, keepdims=True))
        denom = jnp.sum(p, axis=-1, keepdims=True)
        o = jnp.dot(p.astype(BF16), vmem_ref[0, :, sl], preferred_element_type=F32)
        outs.append((o / denom).astype(BF16))
    o_cat = jnp.concatenate(outs, axis=-1)
    o_ref[0] = x1 + jnp.dot(o_cat, wo_ref[...], preferred_element_type=F32)


def _mixmem(x, ug, attn, kmem, vmem, wdw, bdw, lng, lnb, wou, woa, gx, wq, qg, wo):
    b, s, _ = x.shape
    tm = TM_MIX
    tile = lambda w: pl.BlockSpec((1, tm, w), lambda bi, i: (bi, i, 0))
    halo = pl.BlockSpec((1, HALO, CONV_CH),
                        lambda bi, i: (bi, jnp.maximum(i * (tm // HALO) - 1, 0), 0))
    mem = pl.BlockSpec((1, MEM_LEN, D_MODEL), lambda bi, i: (bi, 0, 0))
    consts = [wdw, bdw, lng, lnb, wou, woa, gx, wq, qg, wo]
    return pl.pallas_call(
        _mixmem_kernel,
        grid=(b, s // tm),
        in_specs=[tile(D_MODEL), tile(CONV_CH), halo, tile(MLA_HEADS * MLA_V), mem, mem]
                 + [_const_spec(c.shape) for c in consts],
        out_specs=tile(D_MODEL),
        out_shape=jax.ShapeDtypeStruct((b, s, D_MODEL), F32),
        scratch_shapes=[pltpu.VMEM((HALO + tm, CONV_CH), F32)],
        compiler_params=pltpu.CompilerParams(
            dimension_semantics=("arbitrary", "arbitrary"), vmem_limit_bytes=VMEM_LIMIT),
        name="mixmem",
    )(x, ug, ug, attn, kmem, vmem, *consts)


def _ffn_kernel(x_ref, g_ref, wg_ref, wv_ref, dwg_ref, dwv_ref, bg_ref, bv_ref, wd_ref,
                o_ref, h_ref, prev_ref, ubuf_ref, acc_ref):
    tm = TM_FFN

    @pl.when(pl.program_id(1) == 0)
    def _start_of_sequence():
        prev_ref[...] = jnp.zeros(prev_ref.shape, F32)

    x = x_ref[0]
    h_ref[...] = _rms(x, g_ref[...]).astype(BF16)
    acc_ref[...] = x

    def conv_half(c, half, w_ref, dw_ref, b_ref):
        up = jnp.dot(h_ref[...], w_ref[c], preferred_element_type=F32)
        ubuf_ref[half, 0:FFN_HALO] = prev_ref[half, c]
        ubuf_ref[half, FFN_HALO:] = up
        prev_ref[half, c] = up[tm - FFN_HALO:]
        dw = dw_ref[c]
        return (dw[0:1] * ubuf_ref[half, pl.ds(FFN_HALO - 2, tm), :]
                + dw[1:2] * ubuf_ref[half, pl.ds(FFN_HALO - 1, tm), :]
                + dw[2:3] * up + b_ref[c])

    def chunk(c, carry):
        gate = conv_half(c, 0, wg_ref, dwg_ref, bg_ref)
        val = conv_half(c, 1, wv_ref, dwv_ref, bv_ref)
        act = (gate * jax.nn.sigmoid(gate) * val).astype(BF16)
        acc_ref[...] += jnp.dot(act, wd_ref[c], preferred_element_type=F32)
        return carry

    lax.fori_loop(0, N_FF_CHUNKS, chunk, 0)
    o_ref[0] = acc_ref[...]


def _ffn(x, g, wg, wv, dwg, dwv, bg, bv, wd):
    b, s, _ = x.shape
    tm = TM_FFN
    tile = pl.BlockSpec((1, tm, D_MODEL), lambda bi, i: (bi, i, 0))
    consts = [g, wg, wv, dwg, dwv, bg, bv, wd]
    return pl.pallas_call(
        _ffn_kernel,
        grid=(b, s // tm),
        in_specs=[tile] + [_const_spec(c.shape) for c in consts],
        out_specs=tile,
        out_shape=jax.ShapeDtypeStruct((b, s, D_MODEL), F32),
        scratch_shapes=[
            pltpu.VMEM((tm, D_MODEL), BF16),
            pltpu.VMEM((2, N_FF_CHUNKS, FFN_HALO, FF_CHUNK), F32),
            pltpu.VMEM((2, FFN_HALO + tm, FF_CHUNK), F32),
            pltpu.VMEM((tm, D_MODEL), F32),
        ],
        compiler_params=pltpu.CompilerParams(
            dimension_semantics=("arbitrary", "arbitrary"), vmem_limit_bytes=VMEM_LIMIT),
        name="ffn",
    )(x, *consts)


def _row(v):
    return v.reshape(1, -1).astype(F32)


def _head_slab_row(v):
    return jnp.pad(v.astype(F32), (0, LANES - MLA_QK)).reshape(1, LANES)


def _layer(x, kmem, vmem, pos3, freq, sgn, mix_norm_g, w_in, b_conv_in, w_conv_dw, b_conv_dw, conv_ln_g,
           conv_ln_b, q_lat_norm_g, w_uq, kv_lat_norm_g, w_ukv, q_norm_g, k_norm_g, w_out,
           mem_norm_x_g, w_mem_q, mem_q_norm_g, w_mem_o, ffn_norm_g, w_up, w_ffn_dw, b_ffn_dw, w_down):
    s1 = 2 * CONV_CH
    s2 = s1 + MLA_Q_RANK
    s3 = s2 + MLA_KV_RANK
    w_conv = w_in[:, :s1].astype(BF16)
    w_rope_slab = jnp.pad(w_in[:, s3:], ((0, 0), (MLA_NOPE, LANES - MLA_QK)))
    w_small = jnp.concatenate([w_in[:, s1:s3], w_rope_slab], axis=1).astype(BF16)
    wuq = jnp.pad(w_uq.reshape(MLA_Q_RANK, MLA_HEADS, MLA_QK),
                  ((0, 0), (0, 0), (0, LANES - MLA_QK))).reshape(MLA_Q_RANK, -1).astype(BF16)
    w_ukv3 = w_ukv.reshape(MLA_KV_RANK, MLA_HEADS, MLA_NOPE + MLA_V)
    wuk = jnp.pad(w_ukv3[:, :, :MLA_NOPE],
                  ((0, 0), (0, 0), (0, LANES - MLA_NOPE))).reshape(MLA_KV_RANK, -1).astype(BF16)
    wuv = w_ukv3[:, :, MLA_NOPE:].reshape(MLA_KV_RANK, -1).astype(BF16)
    qg = _head_slab_row(q_norm_g) * (LOG2E / math.sqrt(MLA_QK))
    kg = _head_slab_row(k_norm_g)

    ug, q, k, v = _inproj(x, pos3, _row(mix_norm_g), w_conv, _row(b_conv_in), w_small,
                          _row(q_lat_norm_g), wuq, _row(kv_lat_norm_g), wuk, wuv, qg, kg, freq, sgn)
    attn = _attn(q, k, v)

    w_out_b = w_out.astype(BF16)
    x = _mixmem(x, ug, attn, kmem, vmem, w_conv_dw.astype(F32), _row(b_conv_dw), _row(conv_ln_g),
                _row(conv_ln_b), w_out_b[:CONV_CH], w_out_b[CONV_CH:], _row(mem_norm_x_g),
                w_mem_q.astype(BF16), _row(mem_q_norm_g), w_mem_o.astype(BF16))

    def chunks(w):
        return jnp.moveaxis(w.reshape(w.shape[:-1] + (N_FF_CHUNKS, FF_CHUNK)), -2, 0)

    wg = chunks(w_up[:, :D_FF]).astype(BF16)
    wv = chunks(w_up[:, D_FF:]).astype(BF16)
    dwg = chunks(w_ffn_dw[:, :D_FF]).astype(F32)
    dwv = chunks(w_ffn_dw[:, D_FF:]).astype(F32)
    bg = chunks(b_ffn_dw[None, :D_FF]).astype(F32)
    bv = chunks(b_ffn_dw[None, D_FF:]).astype(F32)
    wd = w_down.reshape(N_FF_CHUNKS, FF_CHUNK, D_MODEL).astype(BF16)
    return _ffn(x, _row(ffn_norm_g), wg, wv, dwg, dwv, bg, bv, wd)


def kernel(x, mem, positions, mix_norm_g, w_in, b_conv_in, w_conv_dw, b_conv_dw, conv_ln_g, conv_ln_b, q_lat_norm_g, w_uq, kv_lat_norm_g, w_ukv, q_norm_g, k_norm_g, w_out, mem_norm_x_g, mem_norm_m_g, w_mem_q, w_mem_kv, mem_q_norm_g, mem_k_norm_g, w_mem_o, ffn_norm_g, w_up, w_ffn_dw, b_ffn_dw, w_down):
    depth = mix_norm_g.shape[0]
    b, s, _ = x.shape
    pos3 = positions.reshape(b, s, 1)
    inv_freq = ROPE_THETA ** (-jnp.arange(0, MLA_ROPE, 2, dtype=F32) / MLA_ROPE)
    half = MLA_ROPE // 2
    zeros = lambda n: jnp.zeros((n,), F32)
    freq = jnp.concatenate([zeros(MLA_NOPE), inv_freq, inv_freq, zeros(LANES - MLA_QK)]).reshape(1, LANES)
    sgn = jnp.concatenate([zeros(MLA_NOPE), -jnp.ones((half,), F32), jnp.ones((half,), F32),
                           zeros(LANES - MLA_QK)]).reshape(1, LANES)
    for l in range(depth):
        kmem, vmem = _memkv(mem, _row(mem_norm_m_g[l]), w_mem_kv[l].astype(BF16), _row(mem_k_norm_g[l]))
        x = _layer(x, kmem, vmem, pos3, freq, sgn, mix_norm_g[l], w_in[l], b_conv_in[l], w_conv_dw[l],
                   b_conv_dw[l], conv_ln_g[l], conv_ln_b[l], q_lat_norm_g[l], w_uq[l], kv_lat_norm_g[l],
                   w_ukv[l], q_norm_g[l], k_norm_g[l], w_out[l], mem_norm_x_g[l], w_mem_q[l],
                   mem_q_norm_g[l], w_mem_o[l], ffn_norm_g[l], w_up[l], w_ffn_dw[l], b_ffn_dw[l], w_down[l])
    return x
```

```python
import functools
import math

import jax
import jax.numpy as jnp
from jax import lax
from jax.experimental import pallas as pl
from jax.experimental.pallas import tpu as pltpu

F32 = jnp.float32
BF16 = jnp.bfloat16

D_MODEL = 1024
CHUNK = 64
EPS = 1e-6
CONV_CH = 512
CONV_WIDTH = 31
MLA_HEADS = 8
MLA_NOPE = 64
MLA_ROPE = 32
MLA_QK = MLA_NOPE + MLA_ROPE
MLA_V = 64
MLA_Q_RANK = 256
MLA_KV_RANK = 128
ROPE_THETA = 10000.0
MEM_LEN = 256
MEM_HEADS = 4
MEM_HEAD_DIM = 256
D_FF = 2816
FFN_CONV_WIDTH = 3

LANES = 128
HALO = 32
FFN_HALO = 8
FF_CHUNK = 256
N_FF_CHUNKS = D_FF // FF_CHUNK
LOG2E = 1.4426950408889634

TM_IN = 512
TM_MIX = 512
TM_FFN = 512
SEQ = 8192
TQ = 512
N_ATTN_TILES = SEQ // TQ
N_ATTN_PAIRS = N_ATTN_TILES * (N_ATTN_TILES + 1) // 2
ATTN_HEADS_PER_STEP = 4
VMEM_LIMIT = 56 * 1024 * 1024


def _rms(x, g):
    return x * lax.rsqrt(jnp.mean(x * x, axis=-1, keepdims=True) + EPS) * g


def _const_spec(shape):
    zeros = (0,) * len(shape)
    return pl.BlockSpec(shape, lambda *_: zeros, pipeline_mode=pl.Buffered(1))


def _memkv_kernel(mem_ref, g_ref, w_ref, kg_ref, k_ref, v_ref):
    hm = _rms(mem_ref[0], g_ref[...]).astype(BF16)
    kv = jnp.dot(hm, w_ref[...], preferred_element_type=F32)
    for h in range(MEM_HEADS):
        sl = slice(h * MEM_HEAD_DIM, (h + 1) * MEM_HEAD_DIM)
        k_ref[0, :, sl] = _rms(kv[:, sl], kg_ref[...]).astype(BF16)
    v_ref[0] = kv[:, D_MODEL:].astype(BF16)


def _memkv(mem, g, w_kv, kg):
    b = mem.shape[0]
    return pl.pallas_call(
        _memkv_kernel,
        grid=(b,),
        in_specs=[
            pl.BlockSpec((1, MEM_LEN, D_MODEL), lambda i: (i, 0, 0)),
            _const_spec((1, D_MODEL)),
            _const_spec((D_MODEL, 2 * D_MODEL)),
            _const_spec((1, MEM_HEAD_DIM)),
        ],
        out_specs=[
            pl.BlockSpec((1, MEM_LEN, D_MODEL), lambda i: (i, 0, 0)),
            pl.BlockSpec((1, MEM_LEN, D_MODEL), lambda i: (i, 0, 0)),
        ],
        out_shape=[jax.ShapeDtypeStruct((b, MEM_LEN, D_MODEL), BF16)] * 2,
        compiler_params=pltpu.CompilerParams(
            dimension_semantics=("arbitrary",), vmem_limit_bytes=VMEM_LIMIT),
        name="memkv",
    )(mem, g, w_kv, kg)


def _inproj_kernel(x_ref, pos_ref, g_ref, wc_ref, bc_ref, ws_ref, gq_ref, wuq_ref, gkv_ref,
                   wuk_ref, wuv_ref, qg_ref, kg_ref, freq_ref, sgn_ref,
                   ug_ref, qt_ref, k_ref, vt_ref):
    h = _rms(x_ref[0], g_ref[...]).astype(BF16)
    zc = jnp.dot(h, wc_ref[...], preferred_element_type=F32) + bc_ref[...]
    ug_ref[0] = zc[:, :CONV_CH] * jax.nn.sigmoid(zc[:, CONV_CH:])

    zs = jnp.dot(h, ws_ref[...], preferred_element_type=F32)
    cqn = _rms(zs[:, :MLA_Q_RANK], gq_ref[...]).astype(BF16)
    kvn = _rms(zs[:, MLA_Q_RANK:MLA_Q_RANK + MLA_KV_RANK], gkv_ref[...]).astype(BF16)
    k_rope = zs[:, MLA_Q_RANK + MLA_KV_RANK:]
    q_all = jnp.dot(cqn, wuq_ref[...], preferred_element_type=F32)
    k_all = jnp.dot(kvn, wuk_ref[...], preferred_element_type=F32)

    ang = pos_ref[0].astype(F32) * freq_ref[...]
    cos_t = jnp.cos(ang)
    sin_t = jnp.sin(ang) * sgn_ref[...]
    first_half = lax.broadcasted_iota(jnp.int32, (1, LANES), 1) < MLA_NOPE + MLA_ROPE // 2

    def norm_rope(slab, gain):
        r = lax.rsqrt(jnp.sum(slab * slab, axis=-1, keepdims=True) * (1.0 / MLA_QK) + EPS)
        y = slab * r * gain
        partner = jnp.where(first_half,
                            pltpu.roll(y, LANES - MLA_ROPE // 2, 1),
                            pltpu.roll(y, MLA_ROPE // 2, 1))
        return y * cos_t + partner * sin_t

    for hd in range(MLA_HEADS):
        sl = slice(hd * LANES, (hd + 1) * LANES)
        qt_ref[0, hd] = norm_rope(q_all[:, sl], qg_ref[...]).T.astype(BF16)
        k_ref[0, hd] = norm_rope(k_all[:, sl] + k_rope, kg_ref[...]).astype(BF16)
    vt_ref[0] = jnp.dot(kvn, wuv_ref[...], preferred_element_type=F32).T.astype(BF16)


def _inproj(x, pos3, g, wc, bc, ws, gq, wuq, gkv, wuk, wuv, qg, kg, freq, sgn):
    b, s, _ = x.shape
    tm = TM_IN
    tile = lambda w: pl.BlockSpec((1, tm, w), lambda bi, i: (bi, i, 0))
    heads = pl.BlockSpec((1, MLA_HEADS, tm, LANES), lambda bi, i: (bi, 0, i, 0))
    heads_t = pl.BlockSpec((1, MLA_HEADS, LANES, tm), lambda bi, i: (bi, 0, 0, i))
    consts = [g, wc, bc, ws, gq, wuq, gkv, wuk, wuv, qg, kg, freq, sgn]
    return pl.pallas_call(
        _inproj_kernel,
        grid=(b, s // tm),
        in_specs=[tile(D_MODEL), tile(1)] + [_const_spec(c.shape) for c in consts],
        out_specs=[tile(CONV_CH), heads_t, heads,
                   pl.BlockSpec((1, MLA_HEADS * MLA_V, tm), lambda bi, i: (bi, 0, i))],
        out_shape=[
            jax.ShapeDtypeStruct((b, s, CONV_CH), F32),
            jax.ShapeDtypeStruct((b, MLA_HEADS, LANES, s), BF16),
            jax.ShapeDtypeStruct((b, MLA_HEADS, s, LANES), BF16),
            jax.ShapeDtypeStruct((b, MLA_HEADS * MLA_V, s), BF16),
        ],
        compiler_params=pltpu.CompilerParams(
            dimension_semantics=("arbitrary", "arbitrary"), vmem_limit_bytes=VMEM_LIMIT),
        name="inproj",
    )(x, pos3, *consts)


def _attn_pairs(nt):
    pairs = [(i, j) for i in range(nt) for j in range(i + 1)]
    pairs.append(pairs[-1])
    return (jnp.asarray([p[0] for p in pairs], jnp.int32),
            jnp.asarray([p[1] for p in pairs], jnp.int32))


def _attn_kernel(qi_ref, kj_ref, qt_ref, k_ref, vt_ref, o_ref,
                 s0_ref, s1_ref, m_ref, al_ref, l_ref, acc_ref):
    n = pl.program_id(2)
    n_pairs = pl.num_programs(2) - 1
    i = qi_ref[n]
    j = kj_ref[n]
    prev_n = jnp.maximum(n - 1, 0)
    prev_was_diag = jnp.logical_and(n > 0, kj_ref[prev_n] == qi_ref[prev_n])

    def phase_a(h, cur_ref, masked, m_old):
        st = jnp.dot(k_ref[0, h], qt_ref[0, h], preferred_element_type=F32)
        if masked:
            key_chunk = lax.broadcasted_iota(jnp.int32, st.shape, 0) // CHUNK
            qry_chunk = lax.broadcasted_iota(jnp.int32, st.shape, 1) // CHUNK
            st = jnp.where(key_chunk <= qry_chunk, st, -jnp.inf)
        cur_ref[h] = st
        m_prev = jnp.where(j == 0, -jnp.inf, m_old)
        m_new = jnp.maximum(m_prev, jnp.max(st, axis=0, keepdims=True))
        m_ref[h] = m_new
        return jnp.exp2(m_prev - m_new)

    def phase_b(h, prev_ref, m_old, alpha):
        pt = jnp.exp2(prev_ref[h] - m_old)
        l_ref[h] = alpha * l_ref[h] + jnp.sum(pt, axis=0, keepdims=True)
        vt = vt_ref[0, h * MLA_V:(h + 1) * MLA_V, :]
        acc_ref[h] = alpha * acc_ref[h] + jnp.dot(vt, pt.astype(BF16), preferred_element_type=F32)

    @pl.when(n == 0)
    def _first():
        l_ref[...] = jnp.zeros(l_ref.shape, F32)
        acc_ref[...] = jnp.zeros(acc_ref.shape, F32)
        for h in range(ATTN_HEADS_PER_STEP):
            al_ref[h] = phase_a(h, s0_ref, True, jnp.zeros((1, TQ), F32))

    for parity, (cur_ref, prev_ref) in enumerate(((s0_ref, s1_ref), (s1_ref, s0_ref))):
        middle = jnp.logical_and(jnp.logical_and(n > 0, n < n_pairs), n % 2 == parity)
        for masked in (False, True):
            @pl.when(jnp.logical_and(middle, (j == i) == masked))
            def _fused(cur_ref=cur_ref, prev_ref=prev_ref, masked=masked):
                for h in range(ATTN_HEADS_PER_STEP):
                    m_old = m_ref[h]
                    alpha = al_ref[h]
                    al_ref[h] = phase_a(h, cur_ref, masked, m_old)
                    phase_b(h, prev_ref, m_old, alpha)

    @pl.when(n == n_pairs)
    def _flush():
        prev_ref = s1_ref if N_ATTN_PAIRS % 2 == 0 else s0_ref
        for h in range(ATTN_HEADS_PER_STEP):
            phase_b(h, prev_ref, m_ref[h], al_ref[h])

    @pl.when(prev_was_diag)
    def _finish_row():
        out_t = jnp.concatenate([acc_ref[h] / l_ref[h] for h in range(ATTN_HEADS_PER_STEP)], axis=0)
        o_ref[0] = out_t.T.astype(BF16)


def _attn(qt, k, vt):
    b, nh, _, s = qt.shape
    assert s // TQ == N_ATTN_TILES
    hps = ATTN_HEADS_PER_STEP
    qi, kj = _attn_pairs(N_ATTN_TILES)
    lag = lambda n: jnp.maximum(n - 1, 0)
    grid_spec = pltpu.PrefetchScalarGridSpec(
        num_scalar_prefetch=2,
        grid=(b, nh // hps, N_ATTN_PAIRS + 1),
        in_specs=[
            pl.BlockSpec((1, hps, LANES, TQ), lambda bi, g, n, qi, kj: (bi, g, 0, qi[n])),
            pl.BlockSpec((1, hps, TQ, LANES), lambda bi, g, n, qi, kj: (bi, g, kj[n], 0)),
            pl.BlockSpec((1, hps * MLA_V, TQ), lambda bi, g, n, qi, kj: (bi, g, kj[lag(n)])),
        ],
        out_specs=pl.BlockSpec((1, TQ, hps * MLA_V), lambda bi, g, n, qi, kj: (bi, qi[lag(n)], g)),
        scratch_shapes=[
            pltpu.VMEM((hps, TQ, TQ), F32),
            pltpu.VMEM((hps, TQ, TQ), F32),
            pltpu.VMEM((hps, 1, TQ), F32),
            pltpu.VMEM((hps, 1, TQ), F32),
            pltpu.VMEM((hps, 1, TQ), F32),
            pltpu.VMEM((hps, MLA_V, TQ), F32),
        ],
    )
    return pl.pallas_call(
        _attn_kernel,
        grid_spec=grid_spec,
        out_shape=jax.ShapeDtypeStruct((b, s, nh * MLA_V), BF16),
        compiler_params=pltpu.CompilerParams(
            dimension_semantics=("arbitrary",) * 3, vmem_limit_bytes=VMEM_LIMIT),
        name="attn",
    )(qi, kj, qt, k, vt)


def _mixmem_kernel(x_ref, ug_ref, halo_ref, attn_ref, kmem_ref, vmem_ref,
                   wdw_ref, bdw_ref, lng_ref, lnb_ref, wou_ref, woa_ref,
                   gx_ref, wq_ref, qg_ref, wo_ref, o_ref, ubuf_ref):
    tm = TM_MIX
    ubuf_ref[0:HALO] = jnp.where(pl.program_id(1) > 0, halo_ref[0], 0.0)
    ubuf_ref[HALO:] = ug_ref[0]
    first = HALO - (CONV_WIDTH - 1)
    y = jnp.zeros((tm, CONV_CH), F32) + bdw_ref[...]
    for t in range(CONV_WIDTH):
        y = y + wdw_ref[t:t + 1, :] * ubuf_ref[pl.ds(first + t, tm), :]
    mu = jnp.mean(y, axis=-1, keepdims=True)
    yc = y - mu
    yn = yc * lax.rsqrt(jnp.mean(yc * yc, axis=-1, keepdims=True) + EPS) * lng_ref[...] + lnb_ref[...]
    u = (yn * jax.nn.sigmoid(yn)).astype(BF16)
    mix = (jnp.dot(u, wou_ref[...], preferred_element_type=F32)
           + jnp.dot(attn_ref[0], woa_ref[...], preferred_element_type=F32))
    x1 = x_ref[0] + mix

    hq = _rms(x1, gx_ref[...]).astype(BF16)
    q = jnp.dot(hq, wq_ref[...], preferred_element_type=F32)
    outs = []
    for h in range(MEM_HEADS):
        sl = slice(h * MEM_HEAD_DIM, (h + 1) * MEM_HEAD_DIM)
        qn = (_rms(q[:, sl], qg_ref[...]) * (LOG2E / math.sqrt(MEM_HEAD_DIM))).astype(BF16)
        s = lax.dot_general(qn, kmem_ref[0, :, sl], (((1,), (1,)), ((), ())),
                            preferred_element_type=F32)
        p = jnp.exp2(s - jnp.max(s, axis=-1, keepdims=True))
        denom = jnp.sum(p, axis=-1, keepdims=True)
        o = jnp.dot(p.astype(BF16), vmem_ref[0, :, sl], preferred_element_type=F32)
        outs.append((o / denom).astype(BF16))
    o_cat = jnp.concatenate(outs, axis=-1)
    o_ref[0] = x1 + jnp.dot(o_cat, wo_ref[...], preferred_element_type=F32)


def _mixmem(x, ug, attn, kmem, vmem, wdw, bdw, lng, lnb, wou, woa, gx, wq, qg, wo):
    b, s, _ = x.shape
    tm = TM_MIX
    tile = lambda w: pl.BlockSpec((1, tm, w), lambda bi, i: (bi, i, 0))
    halo = pl.BlockSpec((1, HALO, CONV_CH),
                        lambda bi, i: (bi, jnp.maximum(i * (tm // HALO) - 1, 0), 0))
    mem = pl.BlockSpec((1, MEM_LEN, D_MODEL), lambda bi, i: (bi, 0, 0))
    consts = [wdw, bdw, lng, lnb, wou, woa, gx, wq, qg, wo]
    return pl.pallas_call(
        _mixmem_kernel,
        grid=(b, s // tm),
        in_specs=[tile(D_MODEL), tile(CONV_CH), halo, tile(MLA_HEADS * MLA_V), mem, mem]
                 + [_const_spec(c.shape) for c in consts],
        out_specs=tile(D_MODEL),
        out_shape=jax.ShapeDtypeStruct((b, s, D_MODEL), F32),
        scratch_shapes=[pltpu.VMEM((HALO + tm, CONV_CH), F32)],
        compiler_params=pltpu.CompilerParams(
            dimension_semantics=("arbitrary", "arbitrary"), vmem_limit_bytes=VMEM_LIMIT),
        name="mixmem",
    )(x, ug, ug, attn, kmem, vmem, *consts)


def _ffn_kernel(x_ref, g_ref, wg_ref, wv_ref, dwg_ref, dwv_ref, bg_ref, bv_ref, wd_ref,
                o_ref, h_ref, prev_ref, ubuf_ref, acc_ref):
    tm = TM_FFN

    @pl.when(pl.program_id(1) == 0)
    def _start_of_sequence():
        prev_ref[...] = jnp.zeros(prev_ref.shape, F32)

    x = x_ref[0]
    h_ref[...] = _rms(x, g_ref[...]).astype(BF16)
    acc_ref[...] = x

    def conv_half(c, half, w_ref, dw_ref, b_ref):
        up = jnp.dot(h_ref[...], w_ref[c], preferred_element_type=F32)
        ubuf_ref[half, 0:FFN_HALO] = prev_ref[half, c]
        ubuf_ref[half, FFN_HALO:] = up
        prev_ref[half, c] = up[tm - FFN_HALO:]
        dw = dw_ref[c]
        return (dw[0:1] * ubuf_ref[half, pl.ds(FFN_HALO - 2, tm), :]
                + dw[1:2] * ubuf_ref[half, pl.ds(FFN_HALO - 1, tm), :]
                + dw[2:3] * up + b_ref[c])

    def chunk(c, carry):
        gate = conv_half(c, 0, wg_ref, dwg_ref, bg_ref)
        val = conv_half(c, 1, wv_ref, dwv_ref, bv_ref)
        act = (gate * jax.nn.sigmoid(gate) * val).astype(BF16)
        acc_ref[...] += jnp.dot(act, wd_ref[c], preferred_element_type=F32)
        return carry

    lax.fori_loop(0, N_FF_CHUNKS, chunk, 0)
    o_ref[0] = acc_ref[...]


def _ffn(x, g, wg, wv, dwg, dwv, bg, bv, wd):
    b, s, _ = x.shape
    tm = TM_FFN
    tile = pl.BlockSpec((1, tm, D_MODEL), lambda bi, i: (bi, i, 0))
    consts = [g, wg, wv, dwg, dwv, bg, bv, wd]
    return pl.pallas_call(
        _ffn_kernel,
        grid=(b, s // tm),
        in_specs=[tile] + [_const_spec(c.shape) for c in consts],
        out_specs=tile,
        out_shape=jax.ShapeDtypeStruct((b, s, D_MODEL), F32),
        scratch_shapes=[
            pltpu.VMEM((tm, D_MODEL), BF16),
            pltpu.VMEM((2, N_FF_CHUNKS, FFN_HALO, FF_CHUNK), F32),
            pltpu.VMEM((2, FFN_HALO + tm, FF_CHUNK), F32),
            pltpu.VMEM((tm, D_MODEL), F32),
        ],
        compiler_params=pltpu.CompilerParams(
            dimension_semantics=("arbitrary", "arbitrary"), vmem_limit_bytes=VMEM_LIMIT),
        name="ffn",
    )(x, *consts)


def _row(v):
    return v.reshape(1, -1).astype(F32)


def _head_slab_row(v):
    return jnp.pad(v.astype(F32), (0, LANES - MLA_QK)).reshape(1, LANES)


def _layer(x, kmem, vmem, pos3, freq, sgn, mix_norm_g, w_in, b_conv_in, w_conv_dw, b_conv_dw, conv_ln_g,
           conv_ln_b, q_lat_norm_g, w_uq, kv_lat_norm_g, w_ukv, q_norm_g, k_norm_g, w_out,
           mem_norm_x_g, w_mem_q, mem_q_norm_g, w_mem_o, ffn_norm_g, w_up, w_ffn_dw, b_ffn_dw, w_down):
    s1 = 2 * CONV_CH
    s2 = s1 + MLA_Q_RANK
    s3 = s2 + MLA_KV_RANK
    w_conv = w_in[:, :s1].astype(BF16)
    w_rope_slab = jnp.pad(w_in[:, s3:], ((0, 0), (MLA_NOPE, LANES - MLA_QK)))
    w_small = jnp.concatenate([w_in[:, s1:s3], w_rope_slab], axis=1).astype(BF16)
    wuq = jnp.pad(w_uq.reshape(MLA_Q_RANK, MLA_HEADS, MLA_QK),
                  ((0, 0), (0, 0), (0, LANES - MLA_QK))).reshape(MLA_Q_RANK, -1).astype(BF16)
    w_ukv3 = w_ukv.reshape(MLA_KV_RANK, MLA_HEADS, MLA_NOPE + MLA_V)
    wuk = jnp.pad(w_ukv3[:, :, :MLA_NOPE],
                  ((0, 0), (0, 0), (0, LANES - MLA_NOPE))).reshape(MLA_KV_RANK, -1).astype(BF16)
    wuv = w_ukv3[:, :, MLA_NOPE:].reshape(MLA_KV_RANK, -1).astype(BF16)
    qg = _head_slab_row(q_norm_g) * (LOG2E / math.sqrt(MLA_QK))
    kg = _head_slab_row(k_norm_g)

    ug, qt, k, vt = _inproj(x, pos3, _row(mix_norm_g), w_conv, _row(b_conv_in), w_small,
                            _row(q_lat_norm_g), wuq, _row(kv_lat_norm_g), wuk, wuv, qg, kg, freq, sgn)
    attn = _attn(qt, k, vt)

    w_out_b = w_out.astype(BF16)
    x = _mixmem(x, ug, attn, kmem, vmem, w_conv_dw.astype(F32), _row(b_conv_dw), _row(conv_ln_g),
                _row(conv_ln_b), w_out_b[:CONV_CH], w_out_b[CONV_CH:], _row(mem_norm_x_g),
                w_mem_q.astype(BF16), _row(mem_q_norm_g), w_mem_o.astype(BF16))

    def chunks(w):
        return jnp.moveaxis(w.reshape(w.shape[:-1] + (N_FF_CHUNKS, FF_CHUNK)), -2, 0)

    wg = chunks(w_up[:, :D_FF]).astype(BF16)
    wv = chunks(w_up[:, D_FF:]).astype(BF16)
    dwg = chunks(w_ffn_dw[:, :D_FF]).astype(F32)
    dwv = chunks(w_ffn_dw[:, D_FF:]).astype(F32)
    bg = chunks(b_ffn_dw[None, :D_FF]).astype(F32)
    bv = chunks(b_ffn_dw[None, D_FF:]).astype(F32)
    wd = w_down.reshape(N_FF_CHUNKS, FF_CHUNK, D_MODEL).astype(BF16)
    return _ffn(x, _row(ffn_norm_g), wg, wv, dwg, dwv, bg, bv, wd)


def kernel(x, mem, positions, mix_norm_g, w_in, b_conv_in, w_conv_dw, b_conv_dw, conv_ln_g, conv_ln_b, q_lat_norm_g, w_uq, kv_lat_norm_g, w_ukv, q_norm_g, k_norm_g, w_out, mem_norm_x_g, mem_norm_m_g, w_mem_q, w_mem_kv, mem_q_norm_g, mem_k_norm_g, w_mem_o, ffn_norm_g, w_up, w_ffn_dw, b_ffn_dw, w_down):
    depth = mix_norm_g.shape[0]
    b, s, _ = x.shape
    pos3 = positions.reshape(b, s, 1)
    inv_freq = ROPE_THETA ** (-jnp.arange(0, MLA_ROPE, 2, dtype=F32) / MLA_ROPE)
    half = MLA_ROPE // 2
    zeros = lambda n: jnp.zeros((n,), F32)
    freq = jnp.concatenate([zeros(MLA_NOPE), inv_freq, inv_freq, zeros(LANES - MLA_QK)]).reshape(1, LANES)
    sgn = jnp.concatenate([zeros(MLA_NOPE), -jnp.ones((half,), F32), jnp.ones((half,), F32),
                           zeros(LANES - MLA_QK)]).reshape(1, LANES)
    for l in range(depth):
        kmem, vmem = _memkv(mem, _row(mem_norm_m_g[l]), w_mem_kv[l].astype(BF16), _row(mem_k_norm_g[l]))
        x = _layer(x, kmem, vmem, pos3, freq, sgn, mix_norm_g[l], w_in[l], b_conv_in[l], w_conv_dw[l],
                   b_conv_dw[l], conv_ln_g[l], conv_ln_b[l], q_lat_norm_g[l], w_uq[l], kv_lat_norm_g[l],
                   w_ukv[l], q_norm_g[l], k_norm_g[l], w_out[l], mem_norm_x_g[l], w_mem_q[l],
                   mem_q_norm_g[l], w_mem_o[l], ffn_norm_g[l], w_up[l], w_ffn_dw[l], b_ffn_dw[l], w_down[l])
    return x
```

```python
import functools
import math

import jax
import jax.numpy as jnp
from jax import lax
from jax.experimental import pallas as pl
from jax.experimental.pallas import tpu as pltpu

F32 = jnp.float32
BF16 = jnp.bfloat16

D_MODEL = 1024
CHUNK = 64
EPS = 1e-6
CONV_CH = 512
CONV_WIDTH = 31
MLA_HEADS = 8
MLA_NOPE = 64
MLA_ROPE = 32
MLA_QK = MLA_NOPE + MLA_ROPE
MLA_V = 64
MLA_Q_RANK = 256
MLA_KV_RANK = 128
IN_COLS = 2 * CONV_CH + MLA_Q_RANK + MLA_KV_RANK + MLA_ROPE
ROPE_THETA = 10000.0
MEM_LEN = 256
MEM_HEADS = 4
MEM_HEAD_DIM = 256
D_FF = 2816
FFN_CONV_WIDTH = 3

LANES = 128
HALO = 32
CONV_ROWS = 128
FFN_HALO = 8
FF_CHUNK = 256
N_FF_CHUNKS = D_FF // FF_CHUNK
LOG2E = 1.4426950408889634

TM_IN = 512
TM_MIX = 512
TM_FFN = 512
SEQ = 8192
TQ = 512
N_ATTN_TILES = SEQ // TQ
N_ATTN_PAIRS = N_ATTN_TILES * (N_ATTN_TILES + 1) // 2
ATTN_HEADS_PER_STEP = 8
VMEM_LIMIT = 56 * 1024 * 1024


def _rms(x, g):
    return x * lax.rsqrt(jnp.mean(x * x, axis=-1, keepdims=True) + EPS) * g


def _const_spec(shape):
    zeros = (0,) * len(shape)
    return pl.BlockSpec(shape, lambda *_: zeros, pipeline_mode=pl.Buffered(1))


def _memkv_kernel(mem_ref, g_ref, w_ref, kg_ref, k_ref, v_ref):
    hm = _rms(mem_ref[0], g_ref[...]).astype(BF16)
    kv = jnp.dot(hm, w_ref[...], preferred_element_type=F32)
    for h in range(MEM_HEADS):
        sl = slice(h * MEM_HEAD_DIM, (h + 1) * MEM_HEAD_DIM)
        k_ref[0, :, sl] = _rms(kv[:, sl], kg_ref[...]).astype(BF16)
    v_ref[0] = kv[:, D_MODEL:].astype(BF16)


def _memkv(mem, g, w_kv, kg):
    b = mem.shape[0]
    return pl.pallas_call(
        _memkv_kernel,
        grid=(b,),
        in_specs=[
            pl.BlockSpec((1, MEM_LEN, D_MODEL), lambda i: (i, 0, 0)),
            _const_spec((1, D_MODEL)),
            _const_spec((D_MODEL, 2 * D_MODEL)),
            _const_spec((1, MEM_HEAD_DIM)),
        ],
        out_specs=[
            pl.BlockSpec((1, MEM_LEN, D_MODEL), lambda i: (i, 0, 0)),
            pl.BlockSpec((1, MEM_LEN, D_MODEL), lambda i: (i, 0, 0)),
        ],
        out_shape=[jax.ShapeDtypeStruct((b, MEM_LEN, D_MODEL), BF16)] * 2,
        compiler_params=pltpu.CompilerParams(
            dimension_semantics=("arbitrary",), vmem_limit_bytes=VMEM_LIMIT),
        name="memkv",
    )(mem, g, w_kv, kg)


def _inproj_kernel(x_ref, pos_ref, g_ref, wc_ref, bc_ref, ws_ref, gq_ref, wuq_ref, gkv_ref,
                   wuk_ref, wuv_ref, qg_ref, kg_ref, freq_ref,
                   ug_ref, qt_ref, k_ref, vt_ref):
    tm = TM_IN
    h = _rms(x_ref[0], g_ref[...]).astype(BF16)
    zc = jnp.dot(h, wc_ref[...], preferred_element_type=F32) + bc_ref[...]
    ug_ref[0] = zc[:, :CONV_CH] * jax.nn.sigmoid(zc[:, CONV_CH:])

    zs_t = jnp.dot(h, ws_ref[...], preferred_element_type=F32).T

    def rms_rows(v, gain):
        return v * lax.rsqrt(jnp.mean(v * v, axis=0, keepdims=True) + EPS) * gain

    cqn_t = rms_rows(zs_t[:MLA_Q_RANK], gq_ref[...]).astype(BF16)
    kvn_t = rms_rows(zs_t[MLA_Q_RANK:MLA_Q_RANK + MLA_KV_RANK], gkv_ref[...]).astype(BF16)
    kr_t = zs_t[MLA_Q_RANK + MLA_KV_RANK:MLA_Q_RANK + MLA_KV_RANK + MLA_ROPE]
    q_t = jnp.dot(wuq_ref[...], cqn_t, preferred_element_type=F32)
    kn_t = jnp.dot(wuk_ref[...], kvn_t, preferred_element_type=F32)
    vt_ref[0] = jnp.dot(wuv_ref[...], kvn_t, preferred_element_type=F32).astype(BF16)

    ang = freq_ref[...] * pos_ref[0].astype(F32)
    cos_t = jnp.cos(ang)
    sin_t = jnp.sin(ang)
    half = MLA_ROPE // 2
    pad = jnp.zeros((LANES - MLA_QK, tm), F32)

    def rope(r):
        x1, x2 = r[:half], r[half:]
        return x1 * cos_t - x2 * sin_t, x1 * sin_t + x2 * cos_t

    kr_sq = jnp.sum(kr_t * kr_t, axis=0, keepdims=True)
    qg = qg_ref[...]
    kg = kg_ref[...]
    for hd in range(MLA_HEADS):
        q_h = q_t[hd * MLA_QK:(hd + 1) * MLA_QK]
        r_q = lax.rsqrt(jnp.sum(q_h * q_h, axis=0, keepdims=True) * (1.0 / MLA_QK) + EPS)
        y = q_h * r_q * qg
        o1, o2 = rope(y[MLA_NOPE:])
        qt_ref[0, hd] = jnp.concatenate([y[:MLA_NOPE], o1, o2, pad], axis=0).astype(BF16)

        k_h = kn_t[hd * MLA_NOPE:(hd + 1) * MLA_NOPE]
        r_k = lax.rsqrt((jnp.sum(k_h * k_h, axis=0, keepdims=True) + kr_sq) * (1.0 / MLA_QK) + EPS)
        o1, o2 = rope(kr_t * r_k * kg[MLA_NOPE:])
        k_slab = jnp.concatenate([k_h * r_k * kg[:MLA_NOPE], o1, o2, pad], axis=0)
        k_ref[0, hd] = k_slab.T.astype(BF16)


def _inproj(x, pos_row, g, wc, bc, ws, gq, wuq, gkv, wuk, wuv, qg, kg, freq):
    b, s, _ = x.shape
    tm = TM_IN
    tile = lambda w: pl.BlockSpec((1, tm, w), lambda bi, i: (bi, i, 0))
    heads = pl.BlockSpec((1, MLA_HEADS, tm, LANES), lambda bi, i: (bi, 0, i, 0))
    heads_t = pl.BlockSpec((1, MLA_HEADS, LANES, tm), lambda bi, i: (bi, 0, 0, i))
    consts = [g, wc, bc, ws, gq, wuq, gkv, wuk, wuv, qg, kg, freq]
    return pl.pallas_call(
        _inproj_kernel,
        grid=(b, s // tm),
        in_specs=[tile(D_MODEL), pl.BlockSpec((1, 1, tm), lambda bi, i: (bi, 0, i))]
                 + [_const_spec(c.shape) for c in consts],
        out_specs=[tile(CONV_CH), heads_t, heads,
                   pl.BlockSpec((1, MLA_HEADS * MLA_V, tm), lambda bi, i: (bi, 0, i))],
        out_shape=[
            jax.ShapeDtypeStruct((b, s, CONV_CH), F32),
            jax.ShapeDtypeStruct((b, MLA_HEADS, LANES, s), BF16),
            jax.ShapeDtypeStruct((b, MLA_HEADS, s, LANES), BF16),
            jax.ShapeDtypeStruct((b, MLA_HEADS * MLA_V, s), BF16),
        ],
        compiler_params=pltpu.CompilerParams(
            dimension_semantics=("arbitrary", "arbitrary"), vmem_limit_bytes=VMEM_LIMIT),
        name="inproj",
    )(x, pos_row, *consts)


def _attn_pairs(nt):
    pairs = [(i, j) for i in range(nt) for j in range(i + 1)]
    pairs.append(pairs[-1])
    return (jnp.asarray([p[0] for p in pairs], jnp.int32),
            jnp.asarray([p[1] for p in pairs], jnp.int32))


def _attn_kernel(qi_ref, kj_ref, qt_ref, k_ref, vt_ref, o_ref,
                 s0_ref, s1_ref, m_ref, al_ref, l_ref, acc_ref):
    n = pl.program_id(2)
    n_pairs = pl.num_programs(2) - 1
    i = qi_ref[n]
    j = kj_ref[n]
    prev_n = jnp.maximum(n - 1, 0)
    prev_was_diag = jnp.logical_and(n > 0, kj_ref[prev_n] == qi_ref[prev_n])

    def phase_a(h, cur_ref, masked, m_old):
        st = jnp.dot(k_ref[0, h], qt_ref[0, h], preferred_element_type=F32)
        if masked:
            key_chunk = lax.broadcasted_iota(jnp.int32, st.shape, 0) // CHUNK
            qry_chunk = lax.broadcasted_iota(jnp.int32, st.shape, 1) // CHUNK
            st = jnp.where(key_chunk <= qry_chunk, st, -jnp.inf)
        cur_ref[h] = st
        m_prev = jnp.where(j == 0, -jnp.inf, m_old)
        m_new = jnp.maximum(m_prev, jnp.max(st, axis=0, keepdims=True))
        m_ref[h] = m_new
        return jnp.exp2(m_prev - m_new)

    def phase_b(h, prev_ref, m_old, alpha):
        pt = jnp.exp2(prev_ref[h] - m_old)
        l_ref[h] = alpha * l_ref[h] + jnp.sum(pt, axis=0, keepdims=True)
        vt = vt_ref[0, h * MLA_V:(h + 1) * MLA_V, :]
        acc_ref[h] = alpha * acc_ref[h] + jnp.dot(vt, pt.astype(BF16), preferred_element_type=F32)

    @pl.when(n == 0)
    def _first():
        l_ref[...] = jnp.zeros(l_ref.shape, F32)
        acc_ref[...] = jnp.zeros(acc_ref.shape, F32)
        for h in range(ATTN_HEADS_PER_STEP):
            al_ref[h] = phase_a(h, s0_ref, True, jnp.zeros((1, TQ), F32))

    for parity, (cur_ref, prev_ref) in enumerate(((s0_ref, s1_ref), (s1_ref, s0_ref))):
        middle = jnp.logical_and(jnp.logical_and(n > 0, n < n_pairs), n % 2 == parity)
        for masked in (False, True):
            @pl.when(jnp.logical_and(middle, (j == i) == masked))
            def _fused(cur_ref=cur_ref, prev_ref=prev_ref, masked=masked):
                for h in range(ATTN_HEADS_PER_STEP):
                    m_old = m_ref[h]
                    alpha = al_ref[h]
                    al_ref[h] = phase_a(h, cur_ref, masked, m_old)
                    phase_b(h, prev_ref, m_old, alpha)

    @pl.when(n == n_pairs)
    def _flush():
        prev_ref = s1_ref if N_ATTN_PAIRS % 2 == 0 else s0_ref
        for h in range(ATTN_HEADS_PER_STEP):
            phase_b(h, prev_ref, m_ref[h], al_ref[h])

    @pl.when(prev_was_diag)
    def _finish_row():
        out_t = jnp.concatenate([acc_ref[h] / l_ref[h] for h in range(ATTN_HEADS_PER_STEP)], axis=0)
        o_ref[0] = out_t.T.astype(BF16)


def _attn(qt, k, vt):
    b, nh, _, s = qt.shape
    assert s // TQ == N_ATTN_TILES
    hps = ATTN_HEADS_PER_STEP
    qi, kj = _attn_pairs(N_ATTN_TILES)
    lag = lambda n: jnp.maximum(n - 1, 0)
    grid_spec = pltpu.PrefetchScalarGridSpec(
        num_scalar_prefetch=2,
        grid=(b, nh // hps, N_ATTN_PAIRS + 1),
        in_specs=[
            pl.BlockSpec((1, hps, LANES, TQ), lambda bi, g, n, qi, kj: (bi, g, 0, qi[n])),
            pl.BlockSpec((1, hps, TQ, LANES), lambda bi, g, n, qi, kj: (bi, g, kj[n], 0)),
            pl.BlockSpec((1, hps * MLA_V, TQ), lambda bi, g, n, qi, kj: (bi, g, kj[lag(n)])),
        ],
        out_specs=pl.BlockSpec((1, TQ, hps * MLA_V), lambda bi, g, n, qi, kj: (bi, qi[lag(n)], g)),
        scratch_shapes=[
            pltpu.VMEM((hps, TQ, TQ), F32),
            pltpu.VMEM((hps, TQ, TQ), F32),
            pltpu.VMEM((hps, 1, TQ), F32),
            pltpu.VMEM((hps, 1, TQ), F32),
            pltpu.VMEM((hps, 1, TQ), F32),
            pltpu.VMEM((hps, MLA_V, TQ), F32),
        ],
    )
    return pl.pallas_call(
        _attn_kernel,
        grid_spec=grid_spec,
        out_shape=jax.ShapeDtypeStruct((b, s, nh * MLA_V), BF16),
        compiler_params=pltpu.CompilerParams(
            dimension_semantics=("arbitrary",) * 3, vmem_limit_bytes=VMEM_LIMIT),
        name="attn",
    )(qi, kj, qt, k, vt)


def _mixmem_kernel(x_ref, ug_ref, halo_ref, attn_ref, kmem_ref, vmem_ref,
                   wdw_ref, bdw_ref, lng_ref, lnb_ref, wou_ref, woa_ref,
                   gx_ref, wq_ref, qg_ref, wo_ref, o_ref, ubuf_ref, y_ref):
    tm = TM_MIX
    n_slabs = CONV_CH // LANES
    for c in range(n_slabs):
        sl = slice(c * LANES, (c + 1) * LANES)
        ubuf_ref[c, 0:HALO] = jnp.where(pl.program_id(1) > 0, halo_ref[0, :, sl], 0.0)
        ubuf_ref[c, HALO:] = ug_ref[0, :, sl]
    first = HALO - (CONV_WIDTH - 1)
    for c in range(n_slabs):
        sl = slice(c * LANES, (c + 1) * LANES)
        for parity in range(2):
            for blk in range(tm // (2 * CONV_ROWS)):
                row0 = parity + 2 * blk * CONV_ROWS
                acc = jnp.zeros((CONV_ROWS, LANES), F32) + bdw_ref[:, sl]
                for t in range(CONV_WIDTH):
                    acc = acc + wdw_ref[t:t + 1, sl] * ubuf_ref[c, pl.ds(first + t + row0, CONV_ROWS, stride=2), :]
                y_ref[c, pl.ds(row0, CONV_ROWS, stride=2), :] = acc
    y = jnp.concatenate([y_ref[c] for c in range(n_slabs)], axis=1)
    mu = jnp.mean(y, axis=-1, keepdims=True)
    yc = y - mu
    yn = yc * lax.rsqrt(jnp.mean(yc * yc, axis=-1, keepdims=True) + EPS) * lng_ref[...] + lnb_ref[...]
    u = (yn * jax.nn.sigmoid(yn)).astype(BF16)
    mix = (jnp.dot(u, wou_ref[...], preferred_element_type=F32)
           + jnp.dot(attn_ref[0], woa_ref[...], preferred_element_type=F32))
    x1 = x_ref[0] + mix

    hq = _rms(x1, gx_ref[...]).astype(BF16)
    q = jnp.dot(hq, wq_ref[...], preferred_element_type=F32)
    outs = []
    for h in range(MEM_HEADS):
        sl = slice(h * MEM_HEAD_DIM, (h + 1) * MEM_HEAD_DIM)
        qn = (_rms(q[:, sl], qg_ref[...]) * (LOG2E / math.sqrt(MEM_HEAD_DIM))).astype(BF16)
        s = lax.dot_general(qn, kmem_ref[0, :, sl], (((1,), (1,)), ((), ())),
                            preferred_element_type=F32)
        p = jnp.exp2(s - jnp.max(s, axis=-1, keepdims=True))
        denom = jnp.sum(p, axis=-1, keepdims=True)
        o = jnp.dot(p.astype(BF16), vmem_ref[0, :, sl], preferred_element_type=F32)
        outs.append((o / denom).astype(BF16))
    o_cat = jnp.concatenate(outs, axis=-1)
    o_ref[0] = x1 + jnp.dot(o_cat, wo_ref[...], preferred_element_type=F32)


def _mixmem(x, ug, attn, kmem, vmem, wdw, bdw, lng, lnb, wou, woa, gx, wq, qg, wo):
    b, s, _ = x.shape
    tm = TM_MIX
    tile = lambda w: pl.BlockSpec((1, tm, w), lambda bi, i: (bi, i, 0))
    halo = pl.BlockSpec((1, HALO, CONV_CH),
                        lambda bi, i: (bi, jnp.maximum(i * (tm // HALO) - 1, 0), 0))
    mem = pl.BlockSpec((1, MEM_LEN, D_MODEL), lambda bi, i: (bi, 0, 0))
    consts = [wdw, bdw, lng, lnb, wou, woa, gx, wq, qg, wo]
    return pl.pallas_call(
        _mixmem_kernel,
        grid=(b, s // tm),
        in_specs=[tile(D_MODEL), tile(CONV_CH), halo, tile(MLA_HEADS * MLA_V), mem, mem]
                 + [_const_spec(c.shape) for c in consts],
        out_specs=tile(D_MODEL),
        out_shape=jax.ShapeDtypeStruct((b, s, D_MODEL), F32),
        scratch_shapes=[pltpu.VMEM((CONV_CH // LANES, HALO + tm, LANES), F32),
                        pltpu.VMEM((CONV_CH // LANES, tm, LANES), F32)],
        compiler_params=pltpu.CompilerParams(
            dimension_semantics=("arbitrary", "arbitrary"), vmem_limit_bytes=VMEM_LIMIT),
        name="mixmem",
    )(x, ug, ug, attn, kmem, vmem, *consts)


def _ffn_kernel(x_ref, g_ref, wg_ref, wv_ref, dwg_ref, dwv_ref, bg_ref, bv_ref, wd_ref,
                o_ref, h_ref, prev_ref, ua_ref, ub_ref, acc_ref):
    tm = TM_FFN

    @pl.when(pl.program_id(1) == 0)
    def _start_of_sequence():
        prev_ref[...] = jnp.zeros(prev_ref.shape, F32)

    x = x_ref[0]
    h_ref[...] = _rms(x, g_ref[...]).astype(BF16)
    acc_ref[...] = x

    def up_proj(c, buf_ref):
        for half, w_ref in enumerate((wg_ref, wv_ref)):
            up = jnp.dot(h_ref[...], w_ref[c], preferred_element_type=F32)
            buf_ref[half, 0:FFN_HALO] = prev_ref[half, c]
            buf_ref[half, FFN_HALO:] = up
            prev_ref[half, c] = up[tm - FFN_HALO:]

    def conv_act_down(c, buf_ref):
        def conv(half, dw_ref, b_ref):
            dw = dw_ref[c]
            return (dw[0:1] * buf_ref[half, pl.ds(FFN_HALO - 2, tm), :]
                    + dw[1:2] * buf_ref[half, pl.ds(FFN_HALO - 1, tm), :]
                    + dw[2:3] * buf_ref[half, pl.ds(FFN_HALO, tm), :] + b_ref[c])
        gate = conv(0, dwg_ref, bg_ref)
        val = conv(1, dwv_ref, bv_ref)
        act = (gate * jax.nn.sigmoid(gate) * val).astype(BF16)
        acc_ref[...] += jnp.dot(act, wd_ref[c], preferred_element_type=F32)

    assert N_FF_CHUNKS % 2 == 1
    up_proj(0, ua_ref)

    def chunk_pair(t, carry):
        c = 2 * t
        up_proj(c + 1, ub_ref)
        conv_act_down(c, ua_ref)
        up_proj(c + 2, ua_ref)
        conv_act_down(c + 1, ub_ref)
        return carry

    lax.fori_loop(0, N_FF_CHUNKS // 2, chunk_pair, 0)
    conv_act_down(N_FF_CHUNKS - 1, ua_ref)
    o_ref[0] = acc_ref[...]


def _ffn(x, g, wg, wv, dwg, dwv, bg, bv, wd):
    b, s, _ = x.shape
    tm = TM_FFN
    tile = pl.BlockSpec((1, tm, D_MODEL), lambda bi, i: (bi, i, 0))
    consts = [g, wg, wv, dwg, dwv, bg, bv, wd]
    return pl.pallas_call(
        _ffn_kernel,
        grid=(b, s // tm),
        in_specs=[tile] + [_const_spec(c.shape) for c in consts],
        out_specs=tile,
        out_shape=jax.ShapeDtypeStruct((b, s, D_MODEL), F32),
        scratch_shapes=[
            pltpu.VMEM((tm, D_MODEL), BF16),
            pltpu.VMEM((2, N_FF_CHUNKS, FFN_HALO, FF_CHUNK), F32),
            pltpu.VMEM((2, FFN_HALO + tm, FF_CHUNK), F32),
            pltpu.VMEM((2, FFN_HALO + tm, FF_CHUNK), F32),
            pltpu.VMEM((tm, D_MODEL), F32),
        ],
        compiler_params=pltpu.CompilerParams(
            dimension_semantics=("arbitrary", "arbitrary"), vmem_limit_bytes=VMEM_LIMIT),
        name="ffn",
    )(x, *consts)


def _row(v):
    return v.reshape(1, -1).astype(F32)


def _col(v, width):
    return jnp.broadcast_to(v.astype(F32)[:, None], (v.shape[0], width))


def _layer(x, kmem, vmem, pos_row, freq, mix_norm_g, w_in, b_conv_in, w_conv_dw, b_conv_dw, conv_ln_g,
           conv_ln_b, q_lat_norm_g, w_uq, kv_lat_norm_g, w_ukv, q_norm_g, k_norm_g, w_out,
           mem_norm_x_g, w_mem_q, mem_q_norm_g, w_mem_o, ffn_norm_g, w_up, w_ffn_dw, b_ffn_dw, w_down):
    s1 = 2 * CONV_CH
    s3 = s1 + MLA_Q_RANK + MLA_KV_RANK
    w_conv = w_in[:, :s1].astype(BF16)
    w_small = jnp.pad(w_in[:, s1:], ((0, 0), (0, 4 * LANES - (IN_COLS - s1)))).astype(BF16)
    wuq_t = w_uq.T.astype(BF16)
    w_ukv3 = w_ukv.reshape(MLA_KV_RANK, MLA_HEADS, MLA_NOPE + MLA_V)
    wuk_t = w_ukv3[:, :, :MLA_NOPE].reshape(MLA_KV_RANK, -1).T.astype(BF16)
    wuv_t = w_ukv3[:, :, MLA_NOPE:].reshape(MLA_KV_RANK, -1).T.astype(BF16)
    qg = _col(q_norm_g * (LOG2E / math.sqrt(MLA_QK)), TM_IN)
    kg = _col(k_norm_g, TM_IN)

    ug, qt, k, vt = _inproj(x, pos_row, _row(mix_norm_g), w_conv, _row(b_conv_in), w_small,
                            _col(q_lat_norm_g, TM_IN), wuq_t, _col(kv_lat_norm_g, TM_IN), wuk_t, wuv_t,
                            qg, kg, freq)
    attn = _attn(qt, k, vt)

    w_out_b = w_out.astype(BF16)
    x = _mixmem(x, ug, attn, kmem, vmem, w_conv_dw.astype(F32), _row(b_conv_dw), _row(conv_ln_g),
                _row(conv_ln_b), w_out_b[:CONV_CH], w_out_b[CONV_CH:], _row(mem_norm_x_g),
                w_mem_q.astype(BF16), _row(mem_q_norm_g), w_mem_o.astype(BF16))

    def chunks(w):
        return jnp.moveaxis(w.reshape(w.shape[:-1] + (N_FF_CHUNKS, FF_CHUNK)), -2, 0)

    wg = chunks(w_up[:, :D_FF]).astype(BF16)
    wv = chunks(w_up[:, D_FF:]).astype(BF16)
    dwg = chunks(w_ffn_dw[:, :D_FF]).astype(F32)
    dwv = chunks(w_ffn_dw[:, D_FF:]).astype(F32)
    bg = chunks(b_ffn_dw[None, :D_FF]).astype(F32)
    bv = chunks(b_ffn_dw[None, D_FF:]).astype(F32)
    wd = w_down.reshape(N_FF_CHUNKS, FF_CHUNK, D_MODEL).astype(BF16)
    return _ffn(x, _row(ffn_norm_g), wg, wv, dwg, dwv, bg, bv, wd)


def kernel(x, mem, positions, mix_norm_g, w_in, b_conv_in, w_conv_dw, b_conv_dw, conv_ln_g, conv_ln_b, q_lat_norm_g, w_uq, kv_lat_norm_g, w_ukv, q_norm_g, k_norm_g, w_out, mem_norm_x_g, mem_norm_m_g, w_mem_q, w_mem_kv, mem_q_norm_g, mem_k_norm_g, w_mem_o, ffn_norm_g, w_up, w_ffn_dw, b_ffn_dw, w_down):
    depth = mix_norm_g.shape[0]
    b, s, _ = x.shape
    pos_row = positions.reshape(b, 1, s)
    inv_freq = ROPE_THETA ** (-jnp.arange(0, MLA_ROPE, 2, dtype=F32) / MLA_ROPE)
    freq = _col(inv_freq, TM_IN)
    for l in range(depth):
        kmem, vmem = _memkv(mem, _row(mem_norm_m_g[l]), w_mem_kv[l].astype(BF16), _row(mem_k_norm_g[l]))
        x = _layer(x, kmem, vmem, pos_row, freq, mix_norm_g[l], w_in[l], b_conv_in[l], w_conv_dw[l],
                   b_conv_dw[l], conv_ln_g[l], conv_ln_b[l], q_lat_norm_g[l], w_uq[l], kv_lat_norm_g[l],
                   w_ukv[l], q_norm_g[l], k_norm_g[l], w_out[l], mem_norm_x_g[l], w_mem_q[l],
                   mem_q_norm_g[l], w_mem_o[l], ffn_norm_g[l], w_up[l], w_ffn_dw[l], b_ffn_dw[l], w_down[l])
    return x
```

```python
import functools
import math

import jax
import jax.numpy as jnp
from jax import lax
from jax.experimental import pallas as pl
from jax.experimental.pallas import tpu as pltpu

F32 = jnp.float32
BF16 = jnp.bfloat16

D_MODEL = 1024
CHUNK = 64
EPS = 1e-6
CONV_CH = 512
CONV_WIDTH = 31
MLA_HEADS = 8
MLA_NOPE = 64
MLA_ROPE = 32
MLA_QK = MLA_NOPE + MLA_ROPE
MLA_V = 64
MLA_VP = 80
MLA_Q_RANK = 256
MLA_KV_RANK = 128
IN_COLS = 2 * CONV_CH + MLA_Q_RANK + MLA_KV_RANK + MLA_ROPE
ROPE_THETA = 10000.0
MEM_LEN = 256
MEM_HEADS = 4
MEM_HEAD_DIM = 256
D_FF = 2816
FFN_CONV_WIDTH = 3

LANES = 128
HALO = 32
CONV_ROWS = 128
FFN_HALO = 8
FF_CHUNK = 256
N_FF_CHUNKS = D_FF // FF_CHUNK
LOG2E = 1.4426950408889634

TM_IN = 512
TM_MIX = 512
TM_FFN = 512
SEQ = 8192
TQ = 512
N_ATTN_TILES = SEQ // TQ
N_ATTN_PAIRS = N_ATTN_TILES * (N_ATTN_TILES + 1) // 2
ATTN_HEADS_PER_STEP = 8
VMEM_LIMIT = 56 * 1024 * 1024


def _rms(x, g):
    return x * lax.rsqrt(jnp.mean(x * x, axis=-1, keepdims=True) + EPS) * g


def _const_spec(shape):
    zeros = (0,) * len(shape)
    return pl.BlockSpec(shape, lambda *_: zeros, pipeline_mode=pl.Buffered(1))


def _memkv_kernel(mem_ref, g_ref, w_ref, kg_ref, k_ref, v_ref):
    hm = _rms(mem_ref[0], g_ref[...]).astype(BF16)
    kv = jnp.dot(hm, w_ref[...], preferred_element_type=F32)
    for h in range(MEM_HEADS):
        sl = slice(h * MEM_HEAD_DIM, (h + 1) * MEM_HEAD_DIM)
        k_ref[0, :, sl] = _rms(kv[:, sl], kg_ref[...]).astype(BF16)
    v_ref[0] = kv[:, D_MODEL:].astype(BF16)


def _memkv(mem, g, w_kv, kg):
    b = mem.shape[0]
    return pl.pallas_call(
        _memkv_kernel,
        grid=(b,),
        in_specs=[
            pl.BlockSpec((1, MEM_LEN, D_MODEL), lambda i: (i, 0, 0)),
            _const_spec((1, D_MODEL)),
            _const_spec((D_MODEL, 2 * D_MODEL)),
            _const_spec((1, MEM_HEAD_DIM)),
        ],
        out_specs=[
            pl.BlockSpec((1, MEM_LEN, D_MODEL), lambda i: (i, 0, 0)),
            pl.BlockSpec((1, MEM_LEN, D_MODEL), lambda i: (i, 0, 0)),
        ],
        out_shape=[jax.ShapeDtypeStruct((b, MEM_LEN, D_MODEL), BF16)] * 2,
        compiler_params=pltpu.CompilerParams(
            dimension_semantics=("arbitrary",), vmem_limit_bytes=VMEM_LIMIT),
        name="memkv",
    )(mem, g, w_kv, kg)


def _inproj_kernel(x_ref, pos_ref, g_ref, wc_ref, bc_ref, ws_ref, gq_ref, wuq_ref, gkv_ref,
                   wuk_ref, wuv_ref, qg_ref, kg_ref, freq_ref,
                   ug_ref, qt_ref, k_ref, vt_ref):
    tm = TM_IN
    h = _rms(x_ref[0], g_ref[...]).astype(BF16)
    zc = jnp.dot(h, wc_ref[...], preferred_element_type=F32) + bc_ref[...]
    ug_ref[0] = zc[:, :CONV_CH] * jax.nn.sigmoid(zc[:, CONV_CH:])

    zs_t = jnp.dot(h, ws_ref[...], preferred_element_type=F32).T

    def rms_rows(v, gain):
        return v * lax.rsqrt(jnp.mean(v * v, axis=0, keepdims=True) + EPS) * gain

    cqn_t = rms_rows(zs_t[:MLA_Q_RANK], gq_ref[...]).astype(BF16)
    kvn_t = rms_rows(zs_t[MLA_Q_RANK:MLA_Q_RANK + MLA_KV_RANK], gkv_ref[...]).astype(BF16)
    kr_t = zs_t[MLA_Q_RANK + MLA_KV_RANK:MLA_Q_RANK + MLA_KV_RANK + MLA_ROPE]
    q_t = jnp.dot(wuq_ref[...], cqn_t, preferred_element_type=F32)
    kn_t = jnp.dot(wuk_ref[...], kvn_t, preferred_element_type=F32)
    v_t = jnp.dot(wuv_ref[...], kvn_t, preferred_element_type=F32)
    ones_rows = (lax.broadcasted_iota(jnp.int32, (MLA_VP - MLA_V, tm), 0) == 0).astype(F32)
    vt_ref[0] = jnp.concatenate(
        [piece for hd in range(MLA_HEADS) for piece in (v_t[hd * MLA_V:(hd + 1) * MLA_V], ones_rows)],
        axis=0).astype(BF16)

    ang = freq_ref[...] * pos_ref[0].astype(F32)
    cos_t = jnp.cos(ang)
    sin_t = jnp.sin(ang)
    half = MLA_ROPE // 2
    pad = jnp.zeros((LANES - MLA_QK, tm), F32)

    def rope(r):
        x1, x2 = r[:half], r[half:]
        return x1 * cos_t - x2 * sin_t, x1 * sin_t + x2 * cos_t

    kr_sq = jnp.sum(kr_t * kr_t, axis=0, keepdims=True)
    qg = qg_ref[...]
    kg = kg_ref[...]
    for hd in range(MLA_HEADS):
        q_h = q_t[hd * MLA_QK:(hd + 1) * MLA_QK]
        r_q = lax.rsqrt(jnp.sum(q_h * q_h, axis=0, keepdims=True) * (1.0 / MLA_QK) + EPS)
        y = q_h * r_q * qg
        o1, o2 = rope(y[MLA_NOPE:])
        qt_ref[0, hd] = jnp.concatenate([y[:MLA_NOPE], o1, o2, pad], axis=0).astype(BF16)

        k_h = kn_t[hd * MLA_NOPE:(hd + 1) * MLA_NOPE]
        r_k = lax.rsqrt((jnp.sum(k_h * k_h, axis=0, keepdims=True) + kr_sq) * (1.0 / MLA_QK) + EPS)
        o1, o2 = rope(kr_t * r_k * kg[MLA_NOPE:])
        k_slab = jnp.concatenate([k_h * r_k * kg[:MLA_NOPE], o1, o2, pad], axis=0)
        k_ref[0, hd] = k_slab.T.astype(BF16)


def _inproj(x, pos_row, g, wc, bc, ws, gq, wuq, gkv, wuk, wuv, qg, kg, freq):
    b, s, _ = x.shape
    tm = TM_IN
    tile = lambda w: pl.BlockSpec((1, tm, w), lambda bi, i: (bi, i, 0))
    heads = pl.BlockSpec((1, MLA_HEADS, tm, LANES), lambda bi, i: (bi, 0, i, 0))
    heads_t = pl.BlockSpec((1, MLA_HEADS, LANES, tm), lambda bi, i: (bi, 0, 0, i))
    consts = [g, wc, bc, ws, gq, wuq, gkv, wuk, wuv, qg, kg, freq]
    return pl.pallas_call(
        _inproj_kernel,
        grid=(b, s // tm),
        in_specs=[tile(D_MODEL), pl.BlockSpec((1, 1, tm), lambda bi, i: (bi, 0, i))]
                 + [_const_spec(c.shape) for c in consts],
        out_specs=[tile(CONV_CH), heads_t, heads,
                   pl.BlockSpec((1, MLA_HEADS * MLA_VP, tm), lambda bi, i: (bi, 0, i))],
        out_shape=[
            jax.ShapeDtypeStruct((b, s, CONV_CH), F32),
            jax.ShapeDtypeStruct((b, MLA_HEADS, LANES, s), BF16),
            jax.ShapeDtypeStruct((b, MLA_HEADS, s, LANES), BF16),
            jax.ShapeDtypeStruct((b, MLA_HEADS * MLA_VP, s), BF16),
        ],
        compiler_params=pltpu.CompilerParams(
            dimension_semantics=("arbitrary", "arbitrary"), vmem_limit_bytes=VMEM_LIMIT),
        name="inproj",
    )(x, pos_row, *consts)


def _attn_pairs(nt):
    pairs = [(i, j) for i in range(nt) for j in range(i + 1)]
    pairs.append(pairs[-1])
    return (jnp.asarray([p[0] for p in pairs], jnp.int32),
            jnp.asarray([p[1] for p in pairs], jnp.int32))


def _attn_kernel(qi_ref, kj_ref, qt_ref, k_ref, vt_ref, o_ref,
                 s0_ref, s1_ref, m_ref, al_ref, acc_ref):
    n = pl.program_id(2)
    n_pairs = pl.num_programs(2) - 1
    i = qi_ref[n]
    j = kj_ref[n]
    prev_n = jnp.maximum(n - 1, 0)
    prev_was_diag = jnp.logical_and(n > 0, kj_ref[prev_n] == qi_ref[prev_n])

    def phase_a(h, cur_ref, masked, m_old):
        st = jnp.dot(k_ref[0, h], qt_ref[0, h], preferred_element_type=F32)
        if masked:
            key_chunk = lax.broadcasted_iota(jnp.int32, st.shape, 0) // CHUNK
            qry_chunk = lax.broadcasted_iota(jnp.int32, st.shape, 1) // CHUNK
            st = jnp.where(key_chunk <= qry_chunk, st, -jnp.inf)
        cur_ref[h] = st
        m_prev = jnp.where(j == 0, -jnp.inf, m_old)
        m_new = jnp.maximum(m_prev, jnp.max(st, axis=0, keepdims=True))
        m_ref[h] = m_new
        return jnp.exp2(m_prev - m_new)

    def phase_b(h, prev_ref, m_old, alpha):
        pt = jnp.exp2(prev_ref[h] - m_old)
        vt = vt_ref[0, h * MLA_VP:(h + 1) * MLA_VP, :]
        acc_ref[h] = alpha * acc_ref[h] + jnp.dot(vt, pt.astype(BF16), preferred_element_type=F32)

    @pl.when(n == 0)
    def _first():
        acc_ref[...] = jnp.zeros(acc_ref.shape, F32)
        for h in range(ATTN_HEADS_PER_STEP):
            al_ref[h] = phase_a(h, s0_ref, True, jnp.zeros((1, TQ), F32))

    for parity, (cur_ref, prev_ref) in enumerate(((s0_ref, s1_ref), (s1_ref, s0_ref))):
        middle = jnp.logical_and(jnp.logical_and(n > 0, n < n_pairs), n % 2 == parity)
        for masked in (False, True):
            @pl.when(jnp.logical_and(middle, (j == i) == masked))
            def _fused(cur_ref=cur_ref, prev_ref=prev_ref, masked=masked):
                for h in range(ATTN_HEADS_PER_STEP):
                    m_old = m_ref[h]
                    alpha = al_ref[h]
                    al_ref[h] = phase_a(h, cur_ref, masked, m_old)
                    phase_b(h, prev_ref, m_old, alpha)

    @pl.when(n == n_pairs)
    def _flush():
        prev_ref = s1_ref if N_ATTN_PAIRS % 2 == 0 else s0_ref
        for h in range(ATTN_HEADS_PER_STEP):
            phase_b(h, prev_ref, m_ref[h], al_ref[h])

    @pl.when(prev_was_diag)
    def _finish_row():
        out_t = jnp.concatenate([acc_ref[h, :MLA_V] / acc_ref[h, MLA_V:MLA_V + 1]
                                 for h in range(ATTN_HEADS_PER_STEP)], axis=0)
        o_ref[0] = out_t.T.astype(BF16)


def _attn(qt, k, vt):
    b, nh, _, s = qt.shape
    assert s // TQ == N_ATTN_TILES
    hps = ATTN_HEADS_PER_STEP
    qi, kj = _attn_pairs(N_ATTN_TILES)
    lag = lambda n: jnp.maximum(n - 1, 0)
    grid_spec = pltpu.PrefetchScalarGridSpec(
        num_scalar_prefetch=2,
        grid=(b, nh // hps, N_ATTN_PAIRS + 1),
        in_specs=[
            pl.BlockSpec((1, hps, LANES, TQ), lambda bi, g, n, qi, kj: (bi, g, 0, qi[n])),
            pl.BlockSpec((1, hps, TQ, LANES), lambda bi, g, n, qi, kj: (bi, g, kj[n], 0)),
            pl.BlockSpec((1, hps * MLA_VP, TQ), lambda bi, g, n, qi, kj: (bi, g, kj[lag(n)])),
        ],
        out_specs=pl.BlockSpec((1, TQ, hps * MLA_V), lambda bi, g, n, qi, kj: (bi, qi[lag(n)], g)),
        scratch_shapes=[
            pltpu.VMEM((hps, TQ, TQ), F32),
            pltpu.VMEM((hps, TQ, TQ), F32),
            pltpu.VMEM((hps, 1, TQ), F32),
            pltpu.VMEM((hps, 1, TQ), F32),
            pltpu.VMEM((hps, MLA_VP, TQ), F32),
        ],
    )
    return pl.pallas_call(
        _attn_kernel,
        grid_spec=grid_spec,
        out_shape=jax.ShapeDtypeStruct((b, s, nh * MLA_V), BF16),
        compiler_params=pltpu.CompilerParams(
            dimension_semantics=("arbitrary",) * 3, vmem_limit_bytes=VMEM_LIMIT),
        name="attn",
    )(qi, kj, qt, k, vt)


def _mixmem_kernel(x_ref, ug_ref, halo_ref, attn_ref, kmem_ref, vmem_ref,
                   wdw_ref, bdw_ref, lng_ref, lnb_ref, wou_ref, woa_ref,
                   gx_ref, wq_ref, qg_ref, wo_ref, o_ref, ubuf_ref, y_ref):
    tm = TM_MIX
    n_slabs = CONV_CH // LANES
    for c in range(n_slabs):
        sl = slice(c * LANES, (c + 1) * LANES)
        ubuf_ref[c, 0:HALO] = jnp.where(pl.program_id(1) > 0, halo_ref[0, :, sl], 0.0)
        ubuf_ref[c, HALO:] = ug_ref[0, :, sl]
    first = HALO - (CONV_WIDTH - 1)
    for c in range(n_slabs):
        sl = slice(c * LANES, (c + 1) * LANES)
        for parity in range(2):
            for blk in range(tm // (2 * CONV_ROWS)):
                row0 = parity + 2 * blk * CONV_ROWS
                acc = jnp.zeros((CONV_ROWS, LANES), F32) + bdw_ref[:, sl]
                for t in range(CONV_WIDTH):
                    acc = acc + wdw_ref[t:t + 1, sl] * ubuf_ref[c, pl.ds(first + t + row0, CONV_ROWS, stride=2), :]
                y_ref[c, pl.ds(row0, CONV_ROWS, stride=2), :] = acc
    y = jnp.concatenate([y_ref[c] for c in range(n_slabs)], axis=1)
    mu = jnp.mean(y, axis=-1, keepdims=True)
    yc = y - mu
    yn = yc * lax.rsqrt(jnp.mean(yc * yc, axis=-1, keepdims=True) + EPS) * lng_ref[...] + lnb_ref[...]
    u = (yn * jax.nn.sigmoid(yn)).astype(BF16)
    mix = (jnp.dot(u, wou_ref[...], preferred_element_type=F32)
           + jnp.dot(attn_ref[0], woa_ref[...], preferred_element_type=F32))
    x1 = x_ref[0] + mix

    hq = _rms(x1, gx_ref[...]).astype(BF16)
    q = jnp.dot(hq, wq_ref[...], preferred_element_type=F32)
    outs = []
    for h in range(MEM_HEADS):
        sl = slice(h * MEM_HEAD_DIM, (h + 1) * MEM_HEAD_DIM)
        qn = (_rms(q[:, sl], qg_ref[...]) * (LOG2E / math.sqrt(MEM_HEAD_DIM))).astype(BF16)
        s = lax.dot_general(qn, kmem_ref[0, :, sl], (((1,), (1,)), ((), ())),
                            preferred_element_type=F32)
        p = jnp.exp2(s - jnp.max(s, axis=-1, keepdims=True))
        denom = jnp.sum(p, axis=-1, keepdims=True)
        o = jnp.dot(p.astype(BF16), vmem_ref[0, :, sl], preferred_element_type=F32)
        outs.append((o / denom).astype(BF16))
    o_cat = jnp.concatenate(outs, axis=-1)
    o_ref[0] = x1 + jnp.dot(o_cat, wo_ref[...], preferred_element_type=F32)


def _mixmem(x, ug, attn, kmem, vmem, wdw, bdw, lng, lnb, wou, woa, gx, wq, qg, wo):
    b, s, _ = x.shape
    tm = TM_MIX
    tile = lambda w: pl.BlockSpec((1, tm, w), lambda bi, i: (bi, i, 0))
    halo = pl.BlockSpec((1, HALO, CONV_CH),
                        lambda bi, i: (bi, jnp.maximum(i * (tm // HALO) - 1, 0), 0))
    mem = pl.BlockSpec((1, MEM_LEN, D_MODEL), lambda bi, i: (bi, 0, 0))
    consts = [wdw, bdw, lng, lnb, wou, woa, gx, wq, qg, wo]
    return pl.pallas_call(
        _mixmem_kernel,
        grid=(b, s // tm),
        in_specs=[tile(D_MODEL), tile(CONV_CH), halo, tile(MLA_HEADS * MLA_V), mem, mem]
                 + [_const_spec(c.shape) for c in consts],
        out_specs=tile(D_MODEL),
        out_shape=jax.ShapeDtypeStruct((b, s, D_MODEL), F32),
        scratch_shapes=[pltpu.VMEM((CONV_CH // LANES, HALO + tm, LANES), F32),
                        pltpu.VMEM((CONV_CH // LANES, tm, LANES), F32)],
        compiler_params=pltpu.CompilerParams(
            dimension_semantics=("arbitrary", "arbitrary"), vmem_limit_bytes=VMEM_LIMIT),
        name="mixmem",
    )(x, ug, ug, attn, kmem, vmem, *consts)


def _ffn_kernel(x_ref, g_ref, wg_ref, wv_ref, dwg_ref, dwv_ref, bg_ref, bv_ref, wd_ref,
                o_ref, h_ref, prev_ref, ua_ref, ub_ref, uc_ref, acc_ref):
    tm = TM_FFN

    @pl.when(pl.program_id(1) == 0)
    def _start_of_sequence():
        prev_ref[...] = jnp.zeros(prev_ref.shape, F32)

    x = x_ref[0]
    h_ref[...] = _rms(x, g_ref[...]).astype(BF16)
    acc_ref[...] = x

    def up_proj(c, buf_ref):
        for half, w_ref in enumerate((wg_ref, wv_ref)):
            up = jnp.dot(h_ref[...], w_ref[c], preferred_element_type=F32)
            buf_ref[half, 0:FFN_HALO] = prev_ref[half, c]
            buf_ref[half, FFN_HALO:] = up
            prev_ref[half, c] = up[tm - FFN_HALO:]

    def conv_act_down(c, buf_ref):
        def conv(half, dw_ref, b_ref):
            dw = dw_ref[c]
            return (dw[0:1] * buf_ref[half, pl.ds(FFN_HALO - 2, tm), :]
                    + dw[1:2] * buf_ref[half, pl.ds(FFN_HALO - 1, tm), :]
                    + dw[2:3] * buf_ref[half, pl.ds(FFN_HALO, tm), :] + b_ref[c])
        gate = conv(0, dwg_ref, bg_ref)
        val = conv(1, dwv_ref, bv_ref)
        act = (gate * jax.nn.sigmoid(gate) * val).astype(BF16)
        acc_ref[...] += jnp.dot(act, wd_ref[c], preferred_element_type=F32)

    bufs = (ua_ref, ub_ref, uc_ref)
    n_rot = (N_FF_CHUNKS - 2) // 3
    assert N_FF_CHUNKS == 3 * n_rot + 2
    up_proj(0, bufs[0])

    def rotation(t, carry):
        c = 3 * t
        for r in range(3):
            up_proj(c + r + 1, bufs[(r + 1) % 3])
            conv_act_down(c + r, bufs[r])
        return carry

    lax.fori_loop(0, n_rot, rotation, 0)
    up_proj(N_FF_CHUNKS - 1, bufs[1])
    conv_act_down(N_FF_CHUNKS - 2, bufs[0])
    conv_act_down(N_FF_CHUNKS - 1, bufs[1])
    o_ref[0] = acc_ref[...]


def _ffn(x, g, wg, wv, dwg, dwv, bg, bv, wd):
    b, s, _ = x.shape
    tm = TM_FFN
    tile = pl.BlockSpec((1, tm, D_MODEL), lambda bi, i: (bi, i, 0))
    consts = [g, wg, wv, dwg, dwv, bg, bv, wd]
    return pl.pallas_call(
        _ffn_kernel,
        grid=(b, s // tm),
        in_specs=[tile] + [_const_spec(c.shape) for c in consts],
        out_specs=tile,
        out_shape=jax.ShapeDtypeStruct((b, s, D_MODEL), F32),
        scratch_shapes=[
            pltpu.VMEM((tm, D_MODEL), BF16),
            pltpu.VMEM((2, N_FF_CHUNKS, FFN_HALO, FF_CHUNK), F32),
            pltpu.VMEM((2, FFN_HALO + tm, FF_CHUNK), F32),
            pltpu.VMEM((2, FFN_HALO + tm, FF_CHUNK), F32),
            pltpu.VMEM((2, FFN_HALO + tm, FF_CHUNK), F32),
            pltpu.VMEM((tm, D_MODEL), F32),
        ],
        compiler_params=pltpu.CompilerParams(
            dimension_semantics=("arbitrary", "arbitrary"), vmem_limit_bytes=VMEM_LIMIT),
        name="ffn",
    )(x, *consts)


def _row(v):
    return v.reshape(1, -1).astype(F32)


def _col(v, width):
    return jnp.broadcast_to(v.astype(F32)[:, None], (v.shape[0], width))


def _layer(x, kmem, vmem, pos_row, freq, mix_norm_g, w_in, b_conv_in, w_conv_dw, b_conv_dw, conv_ln_g,
           conv_ln_b, q_lat_norm_g, w_uq, kv_lat_norm_g, w_ukv, q_norm_g, k_norm_g, w_out,
           mem_norm_x_g, w_mem_q, mem_q_norm_g, w_mem_o, ffn_norm_g, w_up, w_ffn_dw, b_ffn_dw, w_down):
    s1 = 2 * CONV_CH
    s3 = s1 + MLA_Q_RANK + MLA_KV_RANK
    w_conv = w_in[:, :s1].astype(BF16)
    w_small = jnp.pad(w_in[:, s1:], ((0, 0), (0, 4 * LANES - (IN_COLS - s1)))).astype(BF16)
    wuq_t = w_uq.T.astype(BF16)
    w_ukv3 = w_ukv.reshape(MLA_KV_RANK, MLA_HEADS, MLA_NOPE + MLA_V)
    wuk_t = w_ukv3[:, :, :MLA_NOPE].reshape(MLA_KV_RANK, -1).T.astype(BF16)
    wuv_t = w_ukv3[:, :, MLA_NOPE:].reshape(MLA_KV_RANK, -1).T.astype(BF16)
    qg = _col(q_norm_g * (LOG2E / math.sqrt(MLA_QK)), TM_IN)
    kg = _col(k_norm_g, TM_IN)

    ug, qt, k, vt = _inproj(x, pos_row, _row(mix_norm_g), w_conv, _row(b_conv_in), w_small,
                            _col(q_lat_norm_g, TM_IN), wuq_t, _col(kv_lat_norm_g, TM_IN), wuk_t, wuv_t,
                            qg, kg, freq)
    attn = _attn(qt, k, vt)

    w_out_b = w_out.astype(BF16)
    x = _mixmem(x, ug, attn, kmem, vmem, w_conv_dw.astype(F32), _row(b_conv_dw), _row(conv_ln_g),
                _row(conv_ln_b), w_out_b[:CONV_CH], w_out_b[CONV_CH:], _row(mem_norm_x_g),
                w_mem_q.astype(BF16), _row(mem_q_norm_g), w_mem_o.astype(BF16))

    return _ffn_sublayer(x, ffn_norm_g, w_up, w_ffn_dw, b_ffn_dw, w_down)


def _ffn_sublayer(x, ffn_norm_g, w_up, w_ffn_dw, b_ffn_dw, w_down):
    def chunks(w):
        return jnp.moveaxis(w.reshape(w.shape[:-1] + (N_FF_CHUNKS, FF_CHUNK)), -2, 0)

    wg = chunks(w_up[:, :D_FF]).astype(BF16)
    wv = chunks(w_up[:, D_FF:]).astype(BF16)
    dwg = chunks(w_ffn_dw[:, :D_FF]).astype(F32)
    dwv = chunks(w_ffn_dw[:, D_FF:]).astype(F32)
    bg = chunks(b_ffn_dw[None, :D_FF]).astype(F32)
    bv = chunks(b_ffn_dw[None, D_FF:]).astype(F32)
    wd = w_down.reshape(N_FF_CHUNKS, FF_CHUNK, D_MODEL).astype(BF16)
    return _ffn(x, _row(ffn_norm_g), wg, wv, dwg, dwv, bg, bv, wd)


def kernel(x, mem, positions, mix_norm_g, w_in, b_conv_in, w_conv_dw, b_conv_dw, conv_ln_g, conv_ln_b, q_lat_norm_g, w_uq, kv_lat_norm_g, w_ukv, q_norm_g, k_norm_g, w_out, mem_norm_x_g, mem_norm_m_g, w_mem_q, w_mem_kv, mem_q_norm_g, mem_k_norm_g, w_mem_o, ffn_norm_g, w_up, w_ffn_dw, b_ffn_dw, w_down):
    depth = mix_norm_g.shape[0]
    b, s, _ = x.shape
    pos_row = positions.reshape(b, 1, s)
    inv_freq = ROPE_THETA ** (-jnp.arange(0, MLA_ROPE, 2, dtype=F32) / MLA_ROPE)
    freq = _col(inv_freq, TM_IN)
    for l in range(depth):
        kmem, vmem = _memkv(mem, _row(mem_norm_m_g[l]), w_mem_kv[l].astype(BF16), _row(mem_k_norm_g[l]))
        x = _layer(x, kmem, vmem, pos_row, freq, mix_norm_g[l], w_in[l], b_conv_in[l], w_conv_dw[l],
                   b_conv_dw[l], conv_ln_g[l], conv_ln_b[l], q_lat_norm_g[l], w_uq[l], kv_lat_norm_g[l],
                   w_ukv[l], q_norm_g[l], k_norm_g[l], w_out[l], mem_norm_x_g[l], w_mem_q[l],
                   mem_q_norm_g[l], w_mem_o[l], ffn_norm_g[l], w_up[l], w_ffn_dw[l], b_ffn_dw[l], w_down[l])
    return x
```

```python
import functools
import math

import jax
import jax.numpy as jnp
from jax import lax
from jax.experimental import pallas as pl
from jax.experimental.pallas import tpu as pltpu

F32 = jnp.float32
BF16 = jnp.bfloat16

D_MODEL = 1024
CHUNK = 64
EPS = 1e-6
CONV_CH = 512
CONV_WIDTH = 31
MLA_HEADS = 8
MLA_NOPE = 64
MLA_ROPE = 32
MLA_QK = MLA_NOPE + MLA_ROPE
MLA_V = 64
MLA_VP = 80
MLA_Q_RANK = 256
MLA_KV_RANK = 128
IN_COLS = 2 * CONV_CH + MLA_Q_RANK + MLA_KV_RANK + MLA_ROPE
ROPE_THETA = 10000.0
MEM_LEN = 256
MEM_HEADS = 4
MEM_HEAD_DIM = 256
D_FF = 2816
FFN_CONV_WIDTH = 3

LANES = 128
HALO = 32
CONV_ROWS = 128
FFN_HALO = 8
FF_CHUNK = 256
N_FF_CHUNKS = D_FF // FF_CHUNK
LOG2E = 1.4426950408889634

TM_IN = 512
TM_MIX = 512
TM_FFN = 512
SEQ = 8192
TQ = 512
N_ATTN_TILES = SEQ // TQ
N_ATTN_PAIRS = N_ATTN_TILES * (N_ATTN_TILES + 1) // 2
ATTN_HEADS_PER_STEP = 8
VMEM_LIMIT = 56 * 1024 * 1024


def _rms(x, g):
    return x * lax.rsqrt(jnp.mean(x * x, axis=-1, keepdims=True) + EPS) * g


def _const_spec(shape):
    zeros = (0,) * len(shape)
    return pl.BlockSpec(shape, lambda *_: zeros, pipeline_mode=pl.Buffered(1))


def _memkv_kernel(mem_ref, g_ref, w_ref, kg_ref, k_ref, v_ref):
    hm = _rms(mem_ref[0], g_ref[...]).astype(BF16)
    kv = jnp.dot(hm, w_ref[...], preferred_element_type=F32)
    for h in range(MEM_HEADS):
        sl = slice(h * MEM_HEAD_DIM, (h + 1) * MEM_HEAD_DIM)
        k_ref[0, :, sl] = _rms(kv[:, sl], kg_ref[...]).astype(BF16)
    v_ref[0] = kv[:, D_MODEL:].astype(BF16)


def _memkv(mem, g, w_kv, kg):
    b = mem.shape[0]
    return pl.pallas_call(
        _memkv_kernel,
        grid=(b,),
        in_specs=[
            pl.BlockSpec((1, MEM_LEN, D_MODEL), lambda i: (i, 0, 0)),
            _const_spec((1, D_MODEL)),
            _const_spec((D_MODEL, 2 * D_MODEL)),
            _const_spec((1, MEM_HEAD_DIM)),
        ],
        out_specs=[
            pl.BlockSpec((1, MEM_LEN, D_MODEL), lambda i: (i, 0, 0)),
            pl.BlockSpec((1, MEM_LEN, D_MODEL), lambda i: (i, 0, 0)),
        ],
        out_shape=[jax.ShapeDtypeStruct((b, MEM_LEN, D_MODEL), BF16)] * 2,
        compiler_params=pltpu.CompilerParams(
            dimension_semantics=("arbitrary",), vmem_limit_bytes=VMEM_LIMIT),
        name="memkv",
    )(mem, g, w_kv, kg)


def _inproj_kernel(x_ref, pos_ref, g_ref, wc_ref, bc_ref, ws_ref, gq_ref, wuq_ref, gkv_ref,
                   wuk_ref, wuv_ref, qg_ref, kg_ref, freq_ref,
                   ug_ref, qt_ref, k_ref, vt_ref):
    tm = TM_IN
    h = _rms(x_ref[0], g_ref[...]).astype(BF16)
    zc = jnp.dot(h, wc_ref[...], preferred_element_type=F32) + bc_ref[...]
    ug_ref[0] = zc[:, :CONV_CH] * jax.nn.sigmoid(zc[:, CONV_CH:])

    zs_t = jnp.dot(h, ws_ref[...], preferred_element_type=F32).T

    def rms_rows(v, gain):
        return v * lax.rsqrt(jnp.mean(v * v, axis=0, keepdims=True) + EPS) * gain

    cqn_t = rms_rows(zs_t[:MLA_Q_RANK], gq_ref[...]).astype(BF16)
    kvn_t = rms_rows(zs_t[MLA_Q_RANK:MLA_Q_RANK + MLA_KV_RANK], gkv_ref[...]).astype(BF16)
    kr_t = zs_t[MLA_Q_RANK + MLA_KV_RANK:MLA_Q_RANK + MLA_KV_RANK + MLA_ROPE]
    q_t = jnp.dot(wuq_ref[...], cqn_t, preferred_element_type=F32)
    kn_t = jnp.dot(wuk_ref[...], kvn_t, preferred_element_type=F32)
    v_t = jnp.dot(wuv_ref[...], kvn_t, preferred_element_type=F32)
    ones_rows = (lax.broadcasted_iota(jnp.int32, (MLA_VP - MLA_V, tm), 0) == 0).astype(F32)
    vt_ref[0, 0] = jnp.concatenate(
        [piece for hd in range(MLA_HEADS) for piece in (v_t[hd * MLA_V:(hd + 1) * MLA_V], ones_rows)],
        axis=0).astype(BF16)

    ang = freq_ref[...] * pos_ref[0].astype(F32)
    cos_t = jnp.cos(ang)
    sin_t = jnp.sin(ang)
    half = MLA_ROPE // 2
    pad = jnp.zeros((LANES - MLA_QK, tm), F32)

    def rope(r):
        x1, x2 = r[:half], r[half:]
        return x1 * cos_t - x2 * sin_t, x1 * sin_t + x2 * cos_t

    kr_sq = jnp.sum(kr_t * kr_t, axis=0, keepdims=True)
    qg = qg_ref[...]
    kg = kg_ref[...]
    for hd in range(MLA_HEADS):
        q_h = q_t[hd * MLA_QK:(hd + 1) * MLA_QK]
        r_q = lax.rsqrt(jnp.sum(q_h * q_h, axis=0, keepdims=True) * (1.0 / MLA_QK) + EPS)
        y = q_h * r_q * qg
        o1, o2 = rope(y[MLA_NOPE:])
        qt_ref[0, hd] = jnp.concatenate([y[:MLA_NOPE], o1, o2, pad], axis=0).astype(BF16)

        k_h = kn_t[hd * MLA_NOPE:(hd + 1) * MLA_NOPE]
        r_k = lax.rsqrt((jnp.sum(k_h * k_h, axis=0, keepdims=True) + kr_sq) * (1.0 / MLA_QK) + EPS)
        o1, o2 = rope(kr_t * r_k * kg[MLA_NOPE:])
        k_slab = jnp.concatenate([k_h * r_k * kg[:MLA_NOPE], o1, o2, pad], axis=0)
        k_ref[0, hd] = k_slab.T.astype(BF16)


def _inproj(x, pos_row, g, wc, bc, ws, gq, wuq, gkv, wuk, wuv, qg, kg, freq):
    b, s, _ = x.shape
    tm = TM_IN
    tile = lambda w: pl.BlockSpec((1, tm, w), lambda bi, i: (bi, i, 0))
    heads = pl.BlockSpec((1, MLA_HEADS, tm, LANES), lambda bi, i: (bi, 0, i, 0))
    heads_t = pl.BlockSpec((1, MLA_HEADS, LANES, tm), lambda bi, i: (bi, 0, 0, i))
    consts = [g, wc, bc, ws, gq, wuq, gkv, wuk, wuv, qg, kg, freq]
    return pl.pallas_call(
        _inproj_kernel,
        grid=(b, s // tm),
        in_specs=[tile(D_MODEL), pl.BlockSpec((1, 1, tm), lambda bi, i: (bi, 0, i))]
                 + [_const_spec(c.shape) for c in consts],
        out_specs=[tile(CONV_CH), heads_t, heads,
                   pl.BlockSpec((1, 1, MLA_HEADS * MLA_VP, tm), lambda bi, i: (bi, i, 0, 0))],
        out_shape=[
            jax.ShapeDtypeStruct((b, s, CONV_CH), F32),
            jax.ShapeDtypeStruct((b, MLA_HEADS, LANES, s), BF16),
            jax.ShapeDtypeStruct((b, MLA_HEADS, s, LANES), BF16),
            jax.ShapeDtypeStruct((b, s // tm, MLA_HEADS * MLA_VP, tm), BF16),
        ],
        compiler_params=pltpu.CompilerParams(
            dimension_semantics=("arbitrary", "arbitrary"), vmem_limit_bytes=VMEM_LIMIT),
        name="inproj",
    )(x, pos_row, *consts)


def _attn_pairs(nt):
    pairs = [(i, j) for i in range(nt) for j in range(i + 1)]
    pairs.append(pairs[-1])
    return (jnp.asarray([p[0] for p in pairs], jnp.int32),
            jnp.asarray([p[1] for p in pairs], jnp.int32))


def _attn_kernel(qi_ref, kj_ref, qt_ref, k_ref, vt_ref, o_ref,
                 s0_ref, s1_ref, m_ref, al_ref, acc_ref):
    n = pl.program_id(2)
    n_pairs = pl.num_programs(2) - 1
    i = qi_ref[n]
    j = kj_ref[n]
    prev_n = jnp.maximum(n - 1, 0)
    prev_was_diag = jnp.logical_and(n > 0, kj_ref[prev_n] == qi_ref[prev_n])

    key_rows = pl.ds(pl.multiple_of(j * TQ, TQ), TQ)
    prev_j = kj_ref[prev_n]

    def phase_a(h, cur_ref, masked, m_old):
        st = jnp.dot(k_ref[0, h, key_rows, :], qt_ref[0, h], preferred_element_type=F32)
        if masked:
            key_chunk = lax.broadcasted_iota(jnp.int32, st.shape, 0) // CHUNK
            qry_chunk = lax.broadcasted_iota(jnp.int32, st.shape, 1) // CHUNK
            st = jnp.where(key_chunk <= qry_chunk, st, -jnp.inf)
        cur_ref[h] = st
        m_prev = jnp.where(j == 0, -jnp.inf, m_old)
        m_new = jnp.maximum(m_prev, jnp.max(st, axis=0, keepdims=True))
        m_ref[h] = m_new
        return jnp.exp2(m_prev - m_new)

    def phase_b(h, prev_ref, m_old, alpha):
        pt = jnp.exp2(prev_ref[h] - m_old)
        vt = vt_ref[0, prev_j, h * MLA_VP:(h + 1) * MLA_VP, :]
        acc_ref[h] = alpha * acc_ref[h] + jnp.dot(vt, pt.astype(BF16), preferred_element_type=F32)

    @pl.when(n == 0)
    def _first():
        acc_ref[...] = jnp.zeros(acc_ref.shape, F32)
        for h in range(ATTN_HEADS_PER_STEP):
            al_ref[h] = phase_a(h, s0_ref, True, jnp.zeros((1, TQ), F32))

    for parity, (cur_ref, prev_ref) in enumerate(((s0_ref, s1_ref), (s1_ref, s0_ref))):
        middle = jnp.logical_and(jnp.logical_and(n > 0, n < n_pairs), n % 2 == parity)
        for masked in (False, True):
            @pl.when(jnp.logical_and(middle, (j == i) == masked))
            def _fused(cur_ref=cur_ref, prev_ref=prev_ref, masked=masked):
                for h in range(ATTN_HEADS_PER_STEP):
                    m_old = m_ref[h]
                    alpha = al_ref[h]
                    al_ref[h] = phase_a(h, cur_ref, masked, m_old)
                    phase_b(h, prev_ref, m_old, alpha)

    @pl.when(n == n_pairs)
    def _flush():
        prev_ref = s1_ref if N_ATTN_PAIRS % 2 == 0 else s0_ref
        for h in range(ATTN_HEADS_PER_STEP):
            phase_b(h, prev_ref, m_ref[h], al_ref[h])

    @pl.when(prev_was_diag)
    def _finish_row():
        out_t = jnp.concatenate([acc_ref[h, :MLA_V] / acc_ref[h, MLA_V:MLA_V + 1]
                                 for h in range(ATTN_HEADS_PER_STEP)], axis=0)
        o_ref[0] = out_t.T.astype(BF16)


def _attn(qt, k, vt):
    b, nh, _, s = qt.shape
    assert s // TQ == N_ATTN_TILES and nh == ATTN_HEADS_PER_STEP
    hps = ATTN_HEADS_PER_STEP
    qi, kj = _attn_pairs(N_ATTN_TILES)
    lag = lambda n: jnp.maximum(n - 1, 0)
    grid_spec = pltpu.PrefetchScalarGridSpec(
        num_scalar_prefetch=2,
        grid=(b, nh // hps, N_ATTN_PAIRS + 1),
        in_specs=[
            pl.BlockSpec((1, hps, LANES, TQ), lambda bi, g, n, qi, kj: (bi, g, 0, qi[n])),
            pl.BlockSpec((1, hps, s, LANES), lambda bi, g, n, qi, kj: (bi, 0, 0, 0),
                         pipeline_mode=pl.Buffered(1)),
            pl.BlockSpec((1, N_ATTN_TILES, hps * MLA_VP, TQ), lambda bi, g, n, qi, kj: (bi, 0, 0, 0),
                         pipeline_mode=pl.Buffered(1)),
        ],
        out_specs=pl.BlockSpec((1, TQ, hps * MLA_V), lambda bi, g, n, qi, kj: (bi, qi[lag(n)], g)),
        scratch_shapes=[
            pltpu.VMEM((hps, TQ, TQ), F32),
            pltpu.VMEM((hps, TQ, TQ), F32),
            pltpu.VMEM((hps, 1, TQ), F32),
            pltpu.VMEM((hps, 1, TQ), F32),
            pltpu.VMEM((hps, MLA_VP, TQ), F32),
        ],
    )
    return pl.pallas_call(
        _attn_kernel,
        grid_spec=grid_spec,
        out_shape=jax.ShapeDtypeStruct((b, s, nh * MLA_V), BF16),
        compiler_params=pltpu.CompilerParams(
            dimension_semantics=("arbitrary",) * 3, vmem_limit_bytes=VMEM_LIMIT),
        name="attn",
    )(qi, kj, qt, k, vt)


def _mixmem_kernel(x_ref, ug_ref, halo_ref, attn_ref, kmem_ref, vmem_ref,
                   wdw_ref, bdw_ref, lng_ref, lnb_ref, wou_ref, woa_ref,
                   gx_ref, wq_ref, qg_ref, wo_ref, o_ref, ubuf_ref, y_ref):
    tm = TM_MIX
    n_slabs = CONV_CH // LANES
    for c in range(n_slabs):
        sl = slice(c * LANES, (c + 1) * LANES)
        ubuf_ref[c, 0:HALO] = jnp.where(pl.program_id(1) > 0, halo_ref[0, :, sl], 0.0)
        ubuf_ref[c, HALO:] = ug_ref[0, :, sl]
    first = HALO - (CONV_WIDTH - 1)
    for c in range(n_slabs):
        sl = slice(c * LANES, (c + 1) * LANES)
        for parity in range(2):
            for blk in range(tm // (2 * CONV_ROWS)):
                row0 = parity + 2 * blk * CONV_ROWS
                acc = jnp.zeros((CONV_ROWS, LANES), F32) + bdw_ref[:, sl]
                for t in range(CONV_WIDTH):
                    acc = acc + wdw_ref[t:t + 1, sl] * ubuf_ref[c, pl.ds(first + t + row0, CONV_ROWS, stride=2), :]
                y_ref[c, pl.ds(row0, CONV_ROWS, stride=2), :] = acc
    y = jnp.concatenate([y_ref[c] for c in range(n_slabs)], axis=1)
    mu = jnp.mean(y, axis=-1, keepdims=True)
    yc = y - mu
    yn = yc * lax.rsqrt(jnp.mean(yc * yc, axis=-1, keepdims=True) + EPS) * lng_ref[...] + lnb_ref[...]
    u = (yn * jax.nn.sigmoid(yn)).astype(BF16)
    mix = (jnp.dot(u, wou_ref[...], preferred_element_type=F32)
           + jnp.dot(attn_ref[0], woa_ref[...], preferred_element_type=F32))
    x1 = x_ref[0] + mix

    hq = _rms(x1, gx_ref[...]).astype(BF16)
    q = jnp.dot(hq, wq_ref[...], preferred_element_type=F32)
    outs = []
    for h in range(MEM_HEADS):
        sl = slice(h * MEM_HEAD_DIM, (h + 1) * MEM_HEAD_DIM)
        qn = (_rms(q[:, sl], qg_ref[...]) * (LOG2E / math.sqrt(MEM_HEAD_DIM))).astype(BF16)
        s = lax.dot_general(qn, kmem_ref[0, :, sl], (((1,), (1,)), ((), ())),
                            preferred_element_type=F32)
        p = jnp.exp2(s - jnp.max(s, axis=-1, keepdims=True))
        denom = jnp.sum(p, axis=-1, keepdims=True)
        o = jnp.dot(p.astype(BF16), vmem_ref[0, :, sl], preferred_element_type=F32)
        outs.append((o / denom).astype(BF16))
    o_cat = jnp.concatenate(outs, axis=-1)
    o_ref[0] = x1 + jnp.dot(o_cat, wo_ref[...], preferred_element_type=F32)


def _mixmem(x, ug, attn, kmem, vmem, wdw, bdw, lng, lnb, wou, woa, gx, wq, qg, wo):
    b, s, _ = x.shape
    tm = TM_MIX
    tile = lambda w: pl.BlockSpec((1, tm, w), lambda bi, i: (bi, i, 0))
    halo = pl.BlockSpec((1, HALO, CONV_CH),
                        lambda bi, i: (bi, jnp.maximum(i * (tm // HALO) - 1, 0), 0))
    mem = pl.BlockSpec((1, MEM_LEN, D_MODEL), lambda bi, i: (bi, 0, 0))
    consts = [wdw, bdw, lng, lnb, wou, woa, gx, wq, qg, wo]
    return pl.pallas_call(
        _mixmem_kernel,
        grid=(b, s // tm),
        in_specs=[tile(D_MODEL), tile(CONV_CH), halo, tile(MLA_HEADS * MLA_V), mem, mem]
                 + [_const_spec(c.shape) for c in consts],
        out_specs=tile(D_MODEL),
        out_shape=jax.ShapeDtypeStruct((b, s, D_MODEL), F32),
        scratch_shapes=[pltpu.VMEM((CONV_CH // LANES, HALO + tm, LANES), F32),
                        pltpu.VMEM((CONV_CH // LANES, tm, LANES), F32)],
        compiler_params=pltpu.CompilerParams(
            dimension_semantics=("arbitrary", "arbitrary"), vmem_limit_bytes=VMEM_LIMIT),
        name="mixmem",
    )(x, ug, ug, attn, kmem, vmem, *consts)


def _ffn_kernel(x_ref, g_ref, wg_ref, wv_ref, dwg_ref, dwv_ref, bg_ref, bv_ref, wd_ref,
                o_ref, h_ref, prev_ref, ua_ref, ub_ref, uc_ref, acc_ref):
    tm = TM_FFN

    @pl.when(pl.program_id(1) == 0)
    def _start_of_sequence():
        prev_ref[...] = jnp.zeros(prev_ref.shape, F32)

    x = x_ref[0]
    h_ref[...] = _rms(x, g_ref[...]).astype(BF16)
    acc_ref[...] = x

    def up_proj(c, buf_ref):
        for half, w_ref in enumerate((wg_ref, wv_ref)):
            up = jnp.dot(h_ref[...], w_ref[c], preferred_element_type=F32)
            buf_ref[half, 0:FFN_HALO] = prev_ref[half, c]
            buf_ref[half, FFN_HALO:] = up
            prev_ref[half, c] = up[tm - FFN_HALO:]

    def conv_act_down(c, buf_ref):
        def conv(half, dw_ref, b_ref):
            dw = dw_ref[c]
            return (dw[0:1] * buf_ref[half, pl.ds(FFN_HALO - 2, tm), :]
                    + dw[1:2] * buf_ref[half, pl.ds(FFN_HALO - 1, tm), :]
                    + dw[2:3] * buf_ref[half, pl.ds(FFN_HALO, tm), :] + b_ref[c])
        gate = conv(0, dwg_ref, bg_ref)
        val = conv(1, dwv_ref, bv_ref)
        act = (gate * jax.nn.sigmoid(gate) * val).astype(BF16)
        acc_ref[...] += jnp.dot(act, wd_ref[c], preferred_element_type=F32)

    bufs = (ua_ref, ub_ref, uc_ref)
    n_rot = (N_FF_CHUNKS - 2) // 3
    assert N_FF_CHUNKS == 3 * n_rot + 2
    up_proj(0, bufs[0])

    def rotation(t, carry):
        c = 3 * t
        for r in range(3):
            up_proj(c + r + 1, bufs[(r + 1) % 3])
            conv_act_down(c + r, bufs[r])
        return carry

    lax.fori_loop(0, n_rot, rotation, 0)
    up_proj(N_FF_CHUNKS - 1, bufs[1])
    conv_act_down(N_FF_CHUNKS - 2, bufs[0])
    conv_act_down(N_FF_CHUNKS - 1, bufs[1])
    o_ref[0] = acc_ref[...]


def _ffn(x, g, wg, wv, dwg, dwv, bg, bv, wd):
    b, s, _ = x.shape
    tm = TM_FFN
    tile = pl.BlockSpec((1, tm, D_MODEL), lambda bi, i: (bi, i, 0))
    consts = [g, wg, wv, dwg, dwv, bg, bv, wd]
    return pl.pallas_call(
        _ffn_kernel,
        grid=(b, s // tm),
        in_specs=[tile] + [_const_spec(c.shape) for c in consts],
        out_specs=tile,
        out_shape=jax.ShapeDtypeStruct((b, s, D_MODEL), F32),
        scratch_shapes=[
            pltpu.VMEM((tm, D_MODEL), BF16),
            pltpu.VMEM((2, N_FF_CHUNKS, FFN_HALO, FF_CHUNK), F32),
            pltpu.VMEM((2, FFN_HALO + tm, FF_CHUNK), F32),
            pltpu.VMEM((2, FFN_HALO + tm, FF_CHUNK), F32),
            pltpu.VMEM((2, FFN_HALO + tm, FF_CHUNK), F32),
            pltpu.VMEM((tm, D_MODEL), F32),
        ],
        compiler_params=pltpu.CompilerParams(
            dimension_semantics=("arbitrary", "arbitrary"), vmem_limit_bytes=VMEM_LIMIT),
        name="ffn",
    )(x, *consts)


def _row(v):
    return v.reshape(1, -1).astype(F32)


def _col(v, width):
    return jnp.broadcast_to(v.astype(F32)[:, None], (v.shape[0], width))


def _layer(x, kmem, vmem, pos_row, freq, mix_norm_g, w_in, b_conv_in, w_conv_dw, b_conv_dw, conv_ln_g,
           conv_ln_b, q_lat_norm_g, w_uq, kv_lat_norm_g, w_ukv, q_norm_g, k_norm_g, w_out,
           mem_norm_x_g, w_mem_q, mem_q_norm_g, w_mem_o, ffn_norm_g, w_up, w_ffn_dw, b_ffn_dw, w_down):
    s1 = 2 * CONV_CH
    s3 = s1 + MLA_Q_RANK + MLA_KV_RANK
    w_conv = w_in[:, :s1].astype(BF16)
    w_small = jnp.pad(w_in[:, s1:], ((0, 0), (0, 4 * LANES - (IN_COLS - s1)))).astype(BF16)
    wuq_t = w_uq.T.astype(BF16)
    w_ukv3 = w_ukv.reshape(MLA_KV_RANK, MLA_HEADS, MLA_NOPE + MLA_V)
    wuk_t = w_ukv3[:, :, :MLA_NOPE].reshape(MLA_KV_RANK, -1).T.astype(BF16)
    wuv_t = w_ukv3[:, :, MLA_NOPE:].reshape(MLA_KV_RANK, -1).T.astype(BF16)
    qg = _col(q_norm_g * (LOG2E / math.sqrt(MLA_QK)), TM_IN)
    kg = _col(k_norm_g, TM_IN)

    ug, qt, k, vt = _inproj(x, pos_row, _row(mix_norm_g), w_conv, _row(b_conv_in), w_small,
                            _col(q_lat_norm_g, TM_IN), wuq_t, _col(kv_lat_norm_g, TM_IN), wuk_t, wuv_t,
                            qg, kg, freq)
    attn = _attn(qt, k, vt)

    w_out_b = w_out.astype(BF16)
    x = _mixmem(x, ug, attn, kmem, vmem, w_conv_dw.astype(F32), _row(b_conv_dw), _row(conv_ln_g),
                _row(conv_ln_b), w_out_b[:CONV_CH], w_out_b[CONV_CH:], _row(mem_norm_x_g),
                w_mem_q.astype(BF16), _row(mem_q_norm_g), w_mem_o.astype(BF16))

    return _ffn_sublayer(x, ffn_norm_g, w_up, w_ffn_dw, b_ffn_dw, w_down)


def _ffn_sublayer(x, ffn_norm_g, w_up, w_ffn_dw, b_ffn_dw, w_down):
    def chunks(w):
        return jnp.moveaxis(w.reshape(w.shape[:-1] + (N_FF_CHUNKS, FF_CHUNK)), -2, 0)

    wg = chunks(w_up[:, :D_FF]).astype(BF16)
    wv = chunks(w_up[:, D_FF:]).astype(BF16)
    dwg = chunks(w_ffn_dw[:, :D_FF]).astype(F32)
    dwv = chunks(w_ffn_dw[:, D_FF:]).astype(F32)
    bg = chunks(b_ffn_dw[None, :D_FF]).astype(F32)
    bv = chunks(b_ffn_dw[None, D_FF:]).astype(F32)
    wd = w_down.reshape(N_FF_CHUNKS, FF_CHUNK, D_MODEL).astype(BF16)
    return _ffn(x, _row(ffn_norm_g), wg, wv, dwg, dwv, bg, bv, wd)


def kernel(x, mem, positions, mix_norm_g, w_in, b_conv_in, w_conv_dw, b_conv_dw, conv_ln_g, conv_ln_b, q_lat_norm_g, w_uq, kv_lat_norm_g, w_ukv, q_norm_g, k_norm_g, w_out, mem_norm_x_g, mem_norm_m_g, w_mem_q, w_mem_kv, mem_q_norm_g, mem_k_norm_g, w_mem_o, ffn_norm_g, w_up, w_ffn_dw, b_ffn_dw, w_down):
    depth = mix_norm_g.shape[0]
    b, s, _ = x.shape
    pos_row = positions.reshape(b, 1, s)
    inv_freq = ROPE_THETA ** (-jnp.arange(0, MLA_ROPE, 2, dtype=F32) / MLA_ROPE)
    freq = _col(inv_freq, TM_IN)
    for l in range(depth):
        kmem, vmem = _memkv(mem, _row(mem_norm_m_g[l]), w_mem_kv[l].astype(BF16), _row(mem_k_norm_g[l]))
        x = _layer(x, kmem, vmem, pos_row, freq, mix_norm_g[l], w_in[l], b_conv_in[l], w_conv_dw[l],
                   b_conv_dw[l], conv_ln_g[l], conv_ln_b[l], q_lat_norm_g[l], w_uq[l], kv_lat_norm_g[l],
                   w_ukv[l], q_norm_g[l], k_norm_g[l], w_out[l], mem_norm_x_g[l], w_mem_q[l],
                   mem_q_norm_g[l], w_mem_o[l], ffn_norm_g[l], w_up[l], w_ffn_dw[l], b_ffn_dw[l], w_down[l])
    return x
```

```python
import functools
import math

import jax
import jax.numpy as jnp
from jax import lax
from jax.experimental import pallas as pl
from jax.experimental.pallas import tpu as pltpu

F32 = jnp.float32
BF16 = jnp.bfloat16

D_MODEL = 1024
CHUNK = 64
EPS = 1e-6
CONV_CH = 512
CONV_WIDTH = 31
MLA_HEADS = 8
MLA_NOPE = 64
MLA_ROPE = 32
MLA_QK = MLA_NOPE + MLA_ROPE
MLA_V = 64
MLA_VP = 80
MLA_Q_RANK = 256
MLA_KV_RANK = 128
IN_COLS = 2 * CONV_CH + MLA_Q_RANK + MLA_KV_RANK + MLA_ROPE
ROPE_THETA = 10000.0
MEM_LEN = 256
MEM_HEADS = 4
MEM_HEAD_DIM = 256
D_FF = 2816
FFN_CONV_WIDTH = 3

LANES = 128
HALO = 32
CONV_ROWS = 128
FFN_HALO = 8
FF_CHUNK = 256
N_FF_CHUNKS = D_FF // FF_CHUNK
LOG2E = 1.4426950408889634

TM_IN = 512
TM_MIX = 512
TM_FFN = 512
SEQ = 8192
TQ = 512
N_ATTN_TILES = SEQ // TQ
N_ATTN_PAIRS = N_ATTN_TILES * (N_ATTN_TILES + 1) // 2
ATTN_HEADS_PER_STEP = 8
VMEM_LIMIT = 56 * 1024 * 1024


def _rms(x, g):
    return x * lax.rsqrt(jnp.mean(x * x, axis=-1, keepdims=True) + EPS) * g


def _const_spec(shape):
    zeros = (0,) * len(shape)
    return pl.BlockSpec(shape, lambda *_: zeros, pipeline_mode=pl.Buffered(1))


def _memkv_kernel(mem_ref, g_ref, w_ref, kg_ref, k_ref, v_ref):
    hm = _rms(mem_ref[0], g_ref[...]).astype(BF16)
    kv = jnp.dot(hm, w_ref[...], preferred_element_type=F32)
    for h in range(MEM_HEADS):
        sl = slice(h * MEM_HEAD_DIM, (h + 1) * MEM_HEAD_DIM)
        k_ref[0, :, sl] = _rms(kv[:, sl], kg_ref[...]).astype(BF16)
    v_ref[0] = kv[:, D_MODEL:].astype(BF16)


def _memkv(mem, g, w_kv, kg):
    b = mem.shape[0]
    return pl.pallas_call(
        _memkv_kernel,
        grid=(b,),
        in_specs=[
            pl.BlockSpec((1, MEM_LEN, D_MODEL), lambda i: (i, 0, 0)),
            _const_spec((1, D_MODEL)),
            _const_spec((D_MODEL, 2 * D_MODEL)),
            _const_spec((1, MEM_HEAD_DIM)),
        ],
        out_specs=[
            pl.BlockSpec((1, MEM_LEN, D_MODEL), lambda i: (i, 0, 0)),
            pl.BlockSpec((1, MEM_LEN, D_MODEL), lambda i: (i, 0, 0)),
        ],
        out_shape=[jax.ShapeDtypeStruct((b, MEM_LEN, D_MODEL), BF16)] * 2,
        compiler_params=pltpu.CompilerParams(
            dimension_semantics=("arbitrary",), vmem_limit_bytes=VMEM_LIMIT),
        name="memkv",
    )(mem, g, w_kv, kg)


def _inproj_kernel(x_ref, pos_ref, g_ref, wc_ref, bc_ref, ws_ref, gq_ref, wuq_ref, gkv_ref,
                   wuk_ref, wuv_ref, qg_ref, kg_ref, freq_ref,
                   ug_ref, qt_ref, k_ref, vt_ref):
    tm = TM_IN
    h = _rms(x_ref[0], g_ref[...]).astype(BF16)
    zc = jnp.dot(h, wc_ref[...], preferred_element_type=F32) + bc_ref[...]
    ug_ref[0] = zc[:, :CONV_CH] * jax.nn.sigmoid(zc[:, CONV_CH:])

    zs_t = jnp.dot(h, ws_ref[...], preferred_element_type=F32).T

    def rms_rows(v, gain):
        return v * lax.rsqrt(jnp.mean(v * v, axis=0, keepdims=True) + EPS) * gain

    cqn_t = rms_rows(zs_t[:MLA_Q_RANK], gq_ref[...]).astype(BF16)
    kvn_t = rms_rows(zs_t[MLA_Q_RANK:MLA_Q_RANK + MLA_KV_RANK], gkv_ref[...]).astype(BF16)
    kr_t = zs_t[MLA_Q_RANK + MLA_KV_RANK:MLA_Q_RANK + MLA_KV_RANK + MLA_ROPE]
    q_t = jnp.dot(wuq_ref[...], cqn_t, preferred_element_type=F32)
    kn_t = jnp.dot(wuk_ref[...], kvn_t, preferred_element_type=F32)
    v_t = jnp.dot(wuv_ref[...], kvn_t, preferred_element_type=F32)
    ones_rows = (lax.broadcasted_iota(jnp.int32, (MLA_VP - MLA_V, tm), 0) == 0).astype(F32)
    vt_ref[0] = jnp.concatenate(
        [piece for hd in range(MLA_HEADS) for piece in (v_t[hd * MLA_V:(hd + 1) * MLA_V], ones_rows)],
        axis=0).astype(BF16)

    ang = freq_ref[...] * pos_ref[0].astype(F32)
    cos_t = jnp.cos(ang)
    sin_t = jnp.sin(ang)
    half = MLA_ROPE // 2
    pad = jnp.zeros((LANES - MLA_QK, tm), F32)

    def rope(r):
        x1, x2 = r[:half], r[half:]
        return x1 * cos_t - x2 * sin_t, x1 * sin_t + x2 * cos_t

    kr_sq = jnp.sum(kr_t * kr_t, axis=0, keepdims=True)
    qg = qg_ref[...]
    kg = kg_ref[...]
    for hd in range(MLA_HEADS):
        q_h = q_t[hd * MLA_QK:(hd + 1) * MLA_QK]
        r_q = lax.rsqrt(jnp.sum(q_h * q_h, axis=0, keepdims=True) * (1.0 / MLA_QK) + EPS)
        y = q_h * r_q * qg
        o1, o2 = rope(y[MLA_NOPE:])
        qt_ref[0, hd] = jnp.concatenate([y[:MLA_NOPE], o1, o2, pad], axis=0).astype(BF16)

        k_h = kn_t[hd * MLA_NOPE:(hd + 1) * MLA_NOPE]
        r_k = lax.rsqrt((jnp.sum(k_h * k_h, axis=0, keepdims=True) + kr_sq) * (1.0 / MLA_QK) + EPS)
        o1, o2 = rope(kr_t * r_k * kg[MLA_NOPE:])
        k_slab = jnp.concatenate([k_h * r_k * kg[:MLA_NOPE], o1, o2, pad], axis=0)
        k_ref[0, hd] = k_slab.T.astype(BF16)


def _inproj(x, pos_row, g, wc, bc, ws, gq, wuq, gkv, wuk, wuv, qg, kg, freq):
    b, s, _ = x.shape
    tm = TM_IN
    tile = lambda w: pl.BlockSpec((1, tm, w), lambda bi, i: (bi, i, 0))
    heads = pl.BlockSpec((1, MLA_HEADS, tm, LANES), lambda bi, i: (bi, 0, i, 0))
    heads_t = pl.BlockSpec((1, MLA_HEADS, LANES, tm), lambda bi, i: (bi, 0, 0, i))
    consts = [g, wc, bc, ws, gq, wuq, gkv, wuk, wuv, qg, kg, freq]
    return pl.pallas_call(
        _inproj_kernel,
        grid=(b, s // tm),
        in_specs=[tile(D_MODEL), pl.BlockSpec((1, 1, tm), lambda bi, i: (bi, 0, i))]
                 + [_const_spec(c.shape) for c in consts],
        out_specs=[tile(CONV_CH), heads_t, heads,
                   pl.BlockSpec((1, MLA_HEADS * MLA_VP, tm), lambda bi, i: (bi, 0, i))],
        out_shape=[
            jax.ShapeDtypeStruct((b, s, CONV_CH), F32),
            jax.ShapeDtypeStruct((b, MLA_HEADS, LANES, s), BF16),
            jax.ShapeDtypeStruct((b, MLA_HEADS, s, LANES), BF16),
            jax.ShapeDtypeStruct((b, MLA_HEADS * MLA_VP, s), BF16),
        ],
        compiler_params=pltpu.CompilerParams(
            dimension_semantics=("arbitrary", "arbitrary"), vmem_limit_bytes=VMEM_LIMIT),
        name="inproj",
    )(x, pos_row, *consts)


def _attn_pairs(nt):
    pairs = [(i, j) for i in range(nt) for j in range(i + 1)]
    pairs.append(pairs[-1])
    return (jnp.asarray([p[0] for p in pairs], jnp.int32),
            jnp.asarray([p[1] for p in pairs], jnp.int32))


def _attn_kernel(qi_ref, kj_ref, qt_ref, k_ref, vt_ref, o_ref,
                 s0_ref, s1_ref, m_ref, al_ref, acc_ref):
    n = pl.program_id(2)
    n_pairs = pl.num_programs(2) - 1
    i = qi_ref[n]
    j = kj_ref[n]
    prev_n = jnp.maximum(n - 1, 0)
    prev_was_diag = jnp.logical_and(n > 0, kj_ref[prev_n] == qi_ref[prev_n])

    def phase_a(h, cur_ref, masked, m_old):
        st = jnp.dot(k_ref[0, h], qt_ref[0, h], preferred_element_type=F32)
        if masked:
            key_chunk = lax.broadcasted_iota(jnp.int32, st.shape, 0) // CHUNK
            qry_chunk = lax.broadcasted_iota(jnp.int32, st.shape, 1) // CHUNK
            st = jnp.where(key_chunk <= qry_chunk, st, -jnp.inf)
        cur_ref[h] = st
        m_prev = jnp.where(j == 0, -jnp.inf, m_old)
        m_new = jnp.maximum(m_prev, jnp.max(st, axis=0, keepdims=True))
        m_ref[h] = m_new
        return jnp.exp2(m_prev - m_new)

    def phase_b(h, prev_ref, m_old, alpha):
        pt = jnp.exp2(prev_ref[h] - m_old)
        vt = vt_ref[0, h * MLA_VP:(h + 1) * MLA_VP, :]
        acc_ref[h] = alpha * acc_ref[h] + jnp.dot(vt, pt.astype(BF16), preferred_element_type=F32)

    @pl.when(n == 0)
    def _first():
        acc_ref[...] = jnp.zeros(acc_ref.shape, F32)
        for h in range(ATTN_HEADS_PER_STEP):
            al_ref[h] = phase_a(h, s0_ref, True, jnp.zeros((1, TQ), F32))

    for parity, (cur_ref, prev_ref) in enumerate(((s0_ref, s1_ref), (s1_ref, s0_ref))):
        middle = jnp.logical_and(jnp.logical_and(n > 0, n < n_pairs), n % 2 == parity)
        for masked in (False, True):
            @pl.when(jnp.logical_and(middle, (j == i) == masked))
            def _fused(cur_ref=cur_ref, prev_ref=prev_ref, masked=masked):
                for h in range(ATTN_HEADS_PER_STEP):
                    m_old = m_ref[h]
                    alpha = al_ref[h]
                    al_ref[h] = phase_a(h, cur_ref, masked, m_old)
                    phase_b(h, prev_ref, m_old, alpha)

    @pl.when(n == n_pairs)
    def _flush():
        prev_ref = s1_ref if N_ATTN_PAIRS % 2 == 0 else s0_ref
        for h in range(ATTN_HEADS_PER_STEP):
            phase_b(h, prev_ref, m_ref[h], al_ref[h])

    @pl.when(prev_was_diag)
    def _finish_row():
        out_t = jnp.concatenate([acc_ref[h, :MLA_V] / acc_ref[h, MLA_V:MLA_V + 1]
                                 for h in range(ATTN_HEADS_PER_STEP)], axis=0)
        o_ref[0] = out_t.T.astype(BF16)


def _attn(qt, k, vt):
    b, nh, _, s = qt.shape
    assert s // TQ == N_ATTN_TILES
    hps = ATTN_HEADS_PER_STEP
    qi, kj = _attn_pairs(N_ATTN_TILES)
    lag = lambda n: jnp.maximum(n - 1, 0)
    grid_spec = pltpu.PrefetchScalarGridSpec(
        num_scalar_prefetch=2,
        grid=(b, nh // hps, N_ATTN_PAIRS + 1),
        in_specs=[
            pl.BlockSpec((1, hps, LANES, TQ), lambda bi, g, n, qi, kj: (bi, g, 0, qi[n])),
            pl.BlockSpec((1, hps, TQ, LANES), lambda bi, g, n, qi, kj: (bi, g, kj[n], 0)),
            pl.BlockSpec((1, hps * MLA_VP, TQ), lambda bi, g, n, qi, kj: (bi, g, kj[lag(n)])),
        ],
        out_specs=pl.BlockSpec((1, TQ, hps * MLA_V), lambda bi, g, n, qi, kj: (bi, qi[lag(n)], g)),
        scratch_shapes=[
            pltpu.VMEM((hps, TQ, TQ), F32),
            pltpu.VMEM((hps, TQ, TQ), F32),
            pltpu.VMEM((hps, 1, TQ), F32),
            pltpu.VMEM((hps, 1, TQ), F32),
            pltpu.VMEM((hps, MLA_VP, TQ), F32),
        ],
    )
    return pl.pallas_call(
        _attn_kernel,
        grid_spec=grid_spec,
        out_shape=jax.ShapeDtypeStruct((b, s, nh * MLA_V), BF16),
        compiler_params=pltpu.CompilerParams(
            dimension_semantics=("arbitrary",) * 3, vmem_limit_bytes=VMEM_LIMIT),
        name="attn",
    )(qi, kj, qt, k, vt)


def _mixmem_kernel(x_ref, ug_ref, halo_ref, attn_ref, kmem_ref, vmem_ref,
                   wdw_ref, bdw_ref, lng_ref, lnb_ref, wout_ref,
                   gx_ref, wq_ref, qg_ref, wo_ref, o_ref, ubuf_ref, y_ref):
    tm = TM_MIX
    n_slabs = CONV_CH // LANES
    for c in range(n_slabs):
        sl = slice(c * LANES, (c + 1) * LANES)
        ubuf_ref[c, 0:HALO] = jnp.where(pl.program_id(1) > 0, halo_ref[0, :, sl], 0.0)
        ubuf_ref[c, HALO:] = ug_ref[0, :, sl]
    first = HALO - (CONV_WIDTH - 1)
    for c in range(n_slabs):
        sl = slice(c * LANES, (c + 1) * LANES)
        for parity in range(2):
            for blk in range(tm // (2 * CONV_ROWS)):
                row0 = parity + 2 * blk * CONV_ROWS
                acc = jnp.zeros((CONV_ROWS, LANES), F32) + bdw_ref[:, sl]
                for t in range(CONV_WIDTH):
                    acc = acc + wdw_ref[t:t + 1, sl] * ubuf_ref[c, pl.ds(first + t + row0, CONV_ROWS, stride=2), :]
                y_ref[c, pl.ds(row0, CONV_ROWS, stride=2), :] = acc
    y = jnp.concatenate([y_ref[c] for c in range(n_slabs)], axis=1)
    mu = jnp.mean(y, axis=-1, keepdims=True)
    yc = y - mu
    yn = yc * lax.rsqrt(jnp.mean(yc * yc, axis=-1, keepdims=True) + EPS) * lng_ref[...] + lnb_ref[...]
    u = (yn * jax.nn.sigmoid(yn)).astype(BF16)
    mix = (jnp.dot(u, wout_ref[:CONV_CH, :], preferred_element_type=F32)
           + jnp.dot(attn_ref[0], wout_ref[CONV_CH:, :], preferred_element_type=F32))
    x1 = x_ref[0] + mix

    hq = _rms(x1, gx_ref[...]).astype(BF16)
    q = jnp.dot(hq, wq_ref[...], preferred_element_type=F32)
    outs = []
    for h in range(MEM_HEADS):
        sl = slice(h * MEM_HEAD_DIM, (h + 1) * MEM_HEAD_DIM)
        qn = (_rms(q[:, sl], qg_ref[...]) * (LOG2E / math.sqrt(MEM_HEAD_DIM))).astype(BF16)
        s = lax.dot_general(qn, kmem_ref[0, :, sl], (((1,), (1,)), ((), ())),
                            preferred_element_type=F32)
        p = jnp.exp2(s - jnp.max(s, axis=-1, keepdims=True))
        denom = jnp.sum(p, axis=-1, keepdims=True)
        o = jnp.dot(p.astype(BF16), vmem_ref[0, :, sl], preferred_element_type=F32)
        outs.append((o / denom).astype(BF16))
    o_cat = jnp.concatenate(outs, axis=-1)
    o_ref[0] = x1 + jnp.dot(o_cat, wo_ref[...], preferred_element_type=F32)


def _mixmem(x, ug, attn, kmem, vmem, wdw, bdw, lng, lnb, wout, gx, wq, qg, wo):
    b, s, _ = x.shape
    tm = TM_MIX
    tile = lambda w: pl.BlockSpec((1, tm, w), lambda bi, i: (bi, i, 0))
    halo = pl.BlockSpec((1, HALO, CONV_CH),
                        lambda bi, i: (bi, jnp.maximum(i * (tm // HALO) - 1, 0), 0))
    mem = pl.BlockSpec((1, MEM_LEN, D_MODEL), lambda bi, i: (bi, 0, 0))
    consts = [wdw, bdw, lng, lnb, wout, gx, wq, qg, wo]
    return pl.pallas_call(
        _mixmem_kernel,
        grid=(b, s // tm),
        in_specs=[tile(D_MODEL), tile(CONV_CH), halo, tile(MLA_HEADS * MLA_V), mem, mem]
                 + [_const_spec(c.shape) for c in consts],
        out_specs=tile(D_MODEL),
        out_shape=jax.ShapeDtypeStruct((b, s, D_MODEL), F32),
        scratch_shapes=[pltpu.VMEM((CONV_CH // LANES, HALO + tm, LANES), F32),
                        pltpu.VMEM((CONV_CH // LANES, tm, LANES), F32)],
        compiler_params=pltpu.CompilerParams(
            dimension_semantics=("arbitrary", "arbitrary"), vmem_limit_bytes=VMEM_LIMIT),
        name="mixmem",
    )(x, ug, ug, attn, kmem, vmem, *consts)


def _ffn_kernel(x_ref, g_ref, wup_ref, dwg_ref, dwv_ref, bg_ref, bv_ref, wd_ref,
                o_ref, h_ref, prev_ref, ua_ref, ub_ref, uc_ref, acc_ref):
    tm = TM_FFN

    @pl.when(pl.program_id(1) == 0)
    def _start_of_sequence():
        prev_ref[...] = jnp.zeros(prev_ref.shape, F32)

    x = x_ref[0]
    h_ref[...] = _rms(x, g_ref[...]).astype(BF16)
    acc_ref[...] = x

    def up_proj(c, buf_ref):
        for half in range(2):
            cols = pl.ds(pl.multiple_of(c * FF_CHUNK + half * D_FF, FF_CHUNK), FF_CHUNK)
            up = jnp.dot(h_ref[...], wup_ref[:, cols], preferred_element_type=F32)
            buf_ref[half, 0:FFN_HALO] = prev_ref[half, c]
            buf_ref[half, FFN_HALO:] = up
            prev_ref[half, c] = up[tm - FFN_HALO:]

    def conv_act_down(c, buf_ref):
        def conv(half, dw_ref, b_ref):
            dw = dw_ref[c]
            return (dw[0:1] * buf_ref[half, pl.ds(FFN_HALO - 2, tm), :]
                    + dw[1:2] * buf_ref[half, pl.ds(FFN_HALO - 1, tm), :]
                    + dw[2:3] * buf_ref[half, pl.ds(FFN_HALO, tm), :] + b_ref[c])
        gate = conv(0, dwg_ref, bg_ref)
        val = conv(1, dwv_ref, bv_ref)
        act = (gate * jax.nn.sigmoid(gate) * val).astype(BF16)
        acc_ref[...] += jnp.dot(act, wd_ref[c], preferred_element_type=F32)

    bufs = (ua_ref, ub_ref, uc_ref)
    n_rot = (N_FF_CHUNKS - 2) // 3
    assert N_FF_CHUNKS == 3 * n_rot + 2
    up_proj(0, bufs[0])

    def rotation(t, carry):
        c = 3 * t
        for r in range(3):
            up_proj(c + r + 1, bufs[(r + 1) % 3])
            conv_act_down(c + r, bufs[r])
        return carry

    lax.fori_loop(0, n_rot, rotation, 0)
    up_proj(N_FF_CHUNKS - 1, bufs[1])
    conv_act_down(N_FF_CHUNKS - 2, bufs[0])
    conv_act_down(N_FF_CHUNKS - 1, bufs[1])
    o_ref[0] = acc_ref[...]


def _ffn(x, g, wup, dwg, dwv, bg, bv, wd):
    b, s, _ = x.shape
    tm = TM_FFN
    tile = pl.BlockSpec((1, tm, D_MODEL), lambda bi, i: (bi, i, 0))
    consts = [g, wup, dwg, dwv, bg, bv, wd]
    return pl.pallas_call(
        _ffn_kernel,
        grid=(b, s // tm),
        in_specs=[tile] + [_const_spec(c.shape) for c in consts],
        out_specs=tile,
        out_shape=jax.ShapeDtypeStruct((b, s, D_MODEL), F32),
        scratch_shapes=[
            pltpu.VMEM((tm, D_MODEL), BF16),
            pltpu.VMEM((2, N_FF_CHUNKS, FFN_HALO, FF_CHUNK), F32),
            pltpu.VMEM((2, FFN_HALO + tm, FF_CHUNK), F32),
            pltpu.VMEM((2, FFN_HALO + tm, FF_CHUNK), F32),
            pltpu.VMEM((2, FFN_HALO + tm, FF_CHUNK), F32),
            pltpu.VMEM((tm, D_MODEL), F32),
        ],
        compiler_params=pltpu.CompilerParams(
            dimension_semantics=("arbitrary", "arbitrary"), vmem_limit_bytes=VMEM_LIMIT),
        name="ffn",
    )(x, *consts)


def _row(v):
    return v.reshape(1, -1).astype(F32)


def _col(v, width):
    return jnp.broadcast_to(v.astype(F32)[:, None], (v.shape[0], width))


def _layer(x, kmem, vmem, pos_row, freq, mix_norm_g, w_in, b_conv_in, w_conv_dw, b_conv_dw, conv_ln_g,
           conv_ln_b, q_lat_norm_g, w_uq, kv_lat_norm_g, w_ukv, q_norm_g, k_norm_g, w_out,
           mem_norm_x_g, w_mem_q, mem_q_norm_g, w_mem_o, ffn_norm_g, w_up, w_ffn_dw, b_ffn_dw, w_down):
    s1 = 2 * CONV_CH
    s3 = s1 + MLA_Q_RANK + MLA_KV_RANK
    w_conv = w_in[:, :s1].astype(BF16)
    w_small = jnp.pad(w_in[:, s1:], ((0, 0), (0, 4 * LANES - (IN_COLS - s1)))).astype(BF16)
    wuq_t = w_uq.T.astype(BF16)
    w_ukv3 = w_ukv.reshape(MLA_KV_RANK, MLA_HEADS, MLA_NOPE + MLA_V)
    wuk_t = w_ukv3[:, :, :MLA_NOPE].reshape(MLA_KV_RANK, -1).T.astype(BF16)
    wuv_t = w_ukv3[:, :, MLA_NOPE:].reshape(MLA_KV_RANK, -1).T.astype(BF16)
    qg = _col(q_norm_g * (LOG2E / math.sqrt(MLA_QK)), TM_IN)
    kg = _col(k_norm_g, TM_IN)

    ug, qt, k, vt = _inproj(x, pos_row, _row(mix_norm_g), w_conv, _row(b_conv_in), w_small,
                            _col(q_lat_norm_g, TM_IN), wuq_t, _col(kv_lat_norm_g, TM_IN), wuk_t, wuv_t,
                            qg, kg, freq)
    attn = _attn(qt, k, vt)

    x = _mixmem(x, ug, attn, kmem, vmem, w_conv_dw.astype(F32), _row(b_conv_dw), _row(conv_ln_g),
                _row(conv_ln_b), w_out.astype(BF16), _row(mem_norm_x_g),
                w_mem_q.astype(BF16), _row(mem_q_norm_g), w_mem_o.astype(BF16))

    return _ffn_sublayer(x, ffn_norm_g, w_up, w_ffn_dw, b_ffn_dw, w_down)


def _ffn_sublayer(x, ffn_norm_g, w_up, w_ffn_dw, b_ffn_dw, w_down):
    def chunks(w):
        return jnp.moveaxis(w.reshape(w.shape[:-1] + (N_FF_CHUNKS, FF_CHUNK)), -2, 0)

    dwg = chunks(w_ffn_dw[:, :D_FF]).astype(F32)
    dwv = chunks(w_ffn_dw[:, D_FF:]).astype(F32)
    bg = chunks(b_ffn_dw[None, :D_FF]).astype(F32)
    bv = chunks(b_ffn_dw[None, D_FF:]).astype(F32)
    wd = w_down.reshape(N_FF_CHUNKS, FF_CHUNK, D_MODEL).astype(BF16)
    return _ffn(x, _row(ffn_norm_g), w_up.astype(BF16), dwg, dwv, bg, bv, wd)


def kernel(x, mem, positions, mix_norm_g, w_in, b_conv_in, w_conv_dw, b_conv_dw, conv_ln_g, conv_ln_b, q_lat_norm_g, w_uq, kv_lat_norm_g, w_ukv, q_norm_g, k_norm_g, w_out, mem_norm_x_g, mem_norm_m_g, w_mem_q, w_mem_kv, mem_q_norm_g, mem_k_norm_g, w_mem_o, ffn_norm_g, w_up, w_ffn_dw, b_ffn_dw, w_down):
    depth = mix_norm_g.shape[0]
    b, s, _ = x.shape
    pos_row = positions.reshape(b, 1, s)
    inv_freq = ROPE_THETA ** (-jnp.arange(0, MLA_ROPE, 2, dtype=F32) / MLA_ROPE)
    freq = _col(inv_freq, TM_IN)
    for l in range(depth):
        kmem, vmem = _memkv(mem, _row(mem_norm_m_g[l]), w_mem_kv[l].astype(BF16), _row(mem_k_norm_g[l]))
        x = _layer(x, kmem, vmem, pos_row, freq, mix_norm_g[l], w_in[l], b_conv_in[l], w_conv_dw[l],
                   b_conv_dw[l], conv_ln_g[l], conv_ln_b[l], q_lat_norm_g[l], w_uq[l], kv_lat_norm_g[l],
                   w_ukv[l], q_norm_g[l], k_norm_g[l], w_out[l], mem_norm_x_g[l], w_mem_q[l],
                   mem_q_norm_g[l], w_mem_o[l], ffn_norm_g[l], w_up[l], w_ffn_dw[l], b_ffn_dw[l], w_down[l])
    return x
```

```python
import functools
import math

import jax
import jax.numpy as jnp
from jax import lax
from jax.experimental import pallas as pl
from jax.experimental.pallas import tpu as pltpu

F32 = jnp.float32
BF16 = jnp.bfloat16

D_MODEL = 1024
CHUNK = 64
EPS = 1e-6
CONV_CH = 512
CONV_WIDTH = 31
MLA_HEADS = 8
MLA_NOPE = 64
MLA_ROPE = 32
MLA_QK = MLA_NOPE + MLA_ROPE
MLA_V = 64
MLA_VP = 80
MLA_Q_RANK = 256
MLA_KV_RANK = 128
IN_COLS = 2 * CONV_CH + MLA_Q_RANK + MLA_KV_RANK + MLA_ROPE
ROPE_THETA = 10000.0
MEM_LEN = 256
MEM_HEADS = 4
MEM_HEAD_DIM = 256
D_FF = 2816
FFN_CONV_WIDTH = 3

LANES = 128
HALO = 32
CONV_ROWS = 128
FFN_HALO = 8
FF_CHUNK = 256
N_FF_CHUNKS = D_FF // FF_CHUNK
LOG2E = 1.4426950408889634

TM_IN = 512
TM_MIX = 512
TM_FFN = 512
SEQ = 8192
TQ = 512
N_ATTN_TILES = SEQ // TQ
N_ATTN_PAIRS = N_ATTN_TILES * (N_ATTN_TILES + 1) // 2
ATTN_HEADS_PER_STEP = 8
VMEM_LIMIT = 56 * 1024 * 1024


def _rms(x, g):
    return x * lax.rsqrt(jnp.mean(x * x, axis=-1, keepdims=True) + EPS) * g


def _const_spec(shape):
    zeros = (0,) * len(shape)
    return pl.BlockSpec(shape, lambda *_: zeros, pipeline_mode=pl.Buffered(1))


def _memkv_kernel(mem_ref, g_ref, w_ref, kg_ref, k_ref, v_ref):
    hm = _rms(mem_ref[0], g_ref[...]).astype(BF16)
    kv = jnp.dot(hm, w_ref[...], preferred_element_type=F32)
    for h in range(MEM_HEADS):
        sl = slice(h * MEM_HEAD_DIM, (h + 1) * MEM_HEAD_DIM)
        k_ref[0, :, sl] = _rms(kv[:, sl], kg_ref[...]).astype(BF16)
    v_ref[0] = kv[:, D_MODEL:].astype(BF16)


def _memkv(mem, g, w_kv, kg):
    b = mem.shape[0]
    return pl.pallas_call(
        _memkv_kernel,
        grid=(b,),
        in_specs=[
            pl.BlockSpec((1, MEM_LEN, D_MODEL), lambda i: (i, 0, 0)),
            _const_spec((1, D_MODEL)),
            _const_spec((D_MODEL, 2 * D_MODEL)),
            _const_spec((1, MEM_HEAD_DIM)),
        ],
        out_specs=[
            pl.BlockSpec((1, MEM_LEN, D_MODEL), lambda i: (i, 0, 0)),
            pl.BlockSpec((1, MEM_LEN, D_MODEL), lambda i: (i, 0, 0)),
        ],
        out_shape=[jax.ShapeDtypeStruct((b, MEM_LEN, D_MODEL), BF16)] * 2,
        compiler_params=pltpu.CompilerParams(
            dimension_semantics=("arbitrary",), vmem_limit_bytes=VMEM_LIMIT),
        name="memkv",
    )(mem, g, w_kv, kg)


def _inproj_kernel(x_ref, pos_ref, g_ref, wc_ref, bc_ref, ws_ref, gq_ref, wuq_ref, gkv_ref,
                   wuk_ref, wuv_ref, qg_ref, kg_ref, freq_ref,
                   ug_ref, qt_ref, k_ref, vt_ref):
    tm = TM_IN
    h = _rms(x_ref[0], g_ref[...]).astype(BF16)
    zc = jnp.dot(h, wc_ref[...], preferred_element_type=F32) + bc_ref[...]
    ug_ref[0] = zc[:, :CONV_CH] * jax.nn.sigmoid(zc[:, CONV_CH:])

    zs_t = jnp.dot(h, ws_ref[...], preferred_element_type=F32).T

    def rms_rows(v, gain):
        return v * lax.rsqrt(jnp.mean(v * v, axis=0, keepdims=True) + EPS) * gain

    cqn_t = rms_rows(zs_t[:MLA_Q_RANK], gq_ref[...]).astype(BF16)
    kvn_t = rms_rows(zs_t[MLA_Q_RANK:MLA_Q_RANK + MLA_KV_RANK], gkv_ref[...]).astype(BF16)
    kr_t = zs_t[MLA_Q_RANK + MLA_KV_RANK:MLA_Q_RANK + MLA_KV_RANK + MLA_ROPE]
    q_t = jnp.dot(wuq_ref[...], cqn_t, preferred_element_type=F32)
    kn_t = jnp.dot(wuk_ref[...], kvn_t, preferred_element_type=F32)
    v_t = jnp.dot(wuv_ref[...], kvn_t, preferred_element_type=F32)
    ones_rows = (lax.broadcasted_iota(jnp.int32, (MLA_VP - MLA_V, tm), 0) == 0).astype(F32)
    vt_ref[0] = jnp.concatenate(
        [piece for hd in range(MLA_HEADS) for piece in (v_t[hd * MLA_V:(hd + 1) * MLA_V], ones_rows)],
        axis=0).astype(BF16)

    ang = freq_ref[...] * pos_ref[0].astype(F32)
    cos_t = jnp.cos(ang)
    sin_t = jnp.sin(ang)
    half = MLA_ROPE // 2
    pad = jnp.zeros((LANES - MLA_QK, tm), F32)

    def rope(r):
        x1, x2 = r[:half], r[half:]
        return x1 * cos_t - x2 * sin_t, x1 * sin_t + x2 * cos_t

    kr_sq = jnp.sum(kr_t * kr_t, axis=0, keepdims=True)
    qg = qg_ref[...]
    kg = kg_ref[...]
    for hd in range(MLA_HEADS):
        q_h = q_t[hd * MLA_QK:(hd + 1) * MLA_QK]
        r_q = lax.rsqrt(jnp.sum(q_h * q_h, axis=0, keepdims=True) * (1.0 / MLA_QK) + EPS)
        y = q_h * r_q * qg
        o1, o2 = rope(y[MLA_NOPE:])
        qt_ref[0, hd] = jnp.concatenate([y[:MLA_NOPE], o1, o2, pad], axis=0).astype(BF16)

        k_h = kn_t[hd * MLA_NOPE:(hd + 1) * MLA_NOPE]
        r_k = lax.rsqrt((jnp.sum(k_h * k_h, axis=0, keepdims=True) + kr_sq) * (1.0 / MLA_QK) + EPS)
        o1, o2 = rope(kr_t * r_k * kg[MLA_NOPE:])
        k_slab = jnp.concatenate([k_h * r_k * kg[:MLA_NOPE], o1, o2, pad], axis=0)
        k_ref[0, hd] = k_slab.T.astype(BF16)


def _inproj(x, pos_row, g, wc, bc, ws, gq, wuq, gkv, wuk, wuv, qg, kg, freq):
    b, s, _ = x.shape
    tm = TM_IN
    tile = lambda w: pl.BlockSpec((1, tm, w), lambda bi, i: (bi, i, 0))
    heads = pl.BlockSpec((1, MLA_HEADS, tm, LANES), lambda bi, i: (bi, 0, i, 0))
    heads_t = pl.BlockSpec((1, MLA_HEADS, LANES, tm), lambda bi, i: (bi, 0, 0, i))
    consts = [g, wc, bc, ws, gq, wuq, gkv, wuk, wuv, qg, kg, freq]
    return pl.pallas_call(
        _inproj_kernel,
        grid=(b, s // tm),
        in_specs=[tile(D_MODEL), pl.BlockSpec((1, 1, tm), lambda bi, i: (bi, 0, i))]
                 + [_const_spec(c.shape) for c in consts],
        out_specs=[tile(CONV_CH), heads_t, heads,
                   pl.BlockSpec((1, MLA_HEADS * MLA_VP, tm), lambda bi, i: (bi, 0, i))],
        out_shape=[
            jax.ShapeDtypeStruct((b, s, CONV_CH), F32),
            jax.ShapeDtypeStruct((b, MLA_HEADS, LANES, s), BF16),
            jax.ShapeDtypeStruct((b, MLA_HEADS, s, LANES), BF16),
            jax.ShapeDtypeStruct((b, MLA_HEADS * MLA_VP, s), BF16),
        ],
        compiler_params=pltpu.CompilerParams(
            dimension_semantics=("arbitrary", "arbitrary"), vmem_limit_bytes=VMEM_LIMIT),
        name="inproj",
    )(x, pos_row, *consts)


def _attn_pairs(nt):
    pairs = [(i, j) for i in range(nt) for j in range(i + 1)]
    pairs.append(pairs[-1])
    return (jnp.asarray([p[0] for p in pairs], jnp.int32),
            jnp.asarray([p[1] for p in pairs], jnp.int32))


def _attn_kernel(qi_ref, kj_ref, qt_ref, k_ref, vt_ref, o_ref,
                 s0_ref, s1_ref, m_ref, al_ref, acc_ref):
    n = pl.program_id(2)
    n_pairs = pl.num_programs(2) - 1
    i = qi_ref[n]
    j = kj_ref[n]
    prev_n = jnp.maximum(n - 1, 0)
    prev_was_diag = jnp.logical_and(n > 0, kj_ref[prev_n] == qi_ref[prev_n])

    def phase_a(h, cur_ref, masked, m_old):
        st = jnp.dot(k_ref[0, h], qt_ref[0, h], preferred_element_type=F32)
        if masked:
            key_chunk = lax.broadcasted_iota(jnp.int32, st.shape, 0) // CHUNK
            qry_chunk = lax.broadcasted_iota(jnp.int32, st.shape, 1) // CHUNK
            st = jnp.where(key_chunk <= qry_chunk, st, -jnp.inf)
        cur_ref[h, :, :TQ] = st
        m_prev = jnp.where(j == 0, -jnp.inf, m_old)
        m_new = jnp.maximum(m_prev, jnp.max(st, axis=0, keepdims=True))
        m_ref[h] = m_new
        return jnp.exp2(m_prev - m_new)

    def phase_b(h, prev_ref, m_old, alpha):
        pt = jnp.exp2(prev_ref[h, :, :TQ] - m_old)
        vt = vt_ref[0, h * MLA_VP:(h + 1) * MLA_VP, :]
        acc_ref[h] = alpha * acc_ref[h] + jnp.dot(vt, pt.astype(BF16), preferred_element_type=F32)

    @pl.when(n == 0)
    def _first():
        acc_ref[...] = jnp.zeros(acc_ref.shape, F32)
        for h in range(ATTN_HEADS_PER_STEP):
            al_ref[h] = phase_a(h, s0_ref, True, jnp.zeros((1, TQ), F32))

    for parity, (cur_ref, prev_ref) in enumerate(((s0_ref, s1_ref), (s1_ref, s0_ref))):
        middle = jnp.logical_and(jnp.logical_and(n > 0, n < n_pairs), n % 2 == parity)
        for masked in (False, True):
            @pl.when(jnp.logical_and(middle, (j == i) == masked))
            def _fused(cur_ref=cur_ref, prev_ref=prev_ref, masked=masked):
                for h in range(ATTN_HEADS_PER_STEP):
                    m_old = m_ref[h]
                    alpha = al_ref[h]
                    al_ref[h] = phase_a(h, cur_ref, masked, m_old)
                    phase_b(h, prev_ref, m_old, alpha)

    @pl.when(n == n_pairs)
    def _flush():
        prev_ref = s1_ref if N_ATTN_PAIRS % 2 == 0 else s0_ref
        for h in range(ATTN_HEADS_PER_STEP):
            phase_b(h, prev_ref, m_ref[h], al_ref[h])

    @pl.when(prev_was_diag)
    def _finish_row():
        out_t = jnp.concatenate([acc_ref[h, :MLA_V] / acc_ref[h, MLA_V:MLA_V + 1]
                                 for h in range(ATTN_HEADS_PER_STEP)], axis=0)
        o_ref[0] = out_t.T.astype(BF16)


def _attn(qt, k, vt):
    b, nh, _, s = qt.shape
    assert s // TQ == N_ATTN_TILES
    hps = ATTN_HEADS_PER_STEP
    qi, kj = _attn_pairs(N_ATTN_TILES)
    lag = lambda n: jnp.maximum(n - 1, 0)
    grid_spec = pltpu.PrefetchScalarGridSpec(
        num_scalar_prefetch=2,
        grid=(b, nh // hps, N_ATTN_PAIRS + 1),
        in_specs=[
            pl.BlockSpec((1, hps, LANES, TQ), lambda bi, g, n, qi, kj: (bi, g, 0, qi[n])),
            pl.BlockSpec((1, hps, TQ, LANES), lambda bi, g, n, qi, kj: (bi, g, kj[n], 0)),
            pl.BlockSpec((1, hps * MLA_VP, TQ), lambda bi, g, n, qi, kj: (bi, g, kj[lag(n)])),
        ],
        out_specs=pl.BlockSpec((1, TQ, hps * MLA_V), lambda bi, g, n, qi, kj: (bi, qi[lag(n)], g)),
        scratch_shapes=[
            pltpu.VMEM((hps, TQ, TQ + LANES), F32),
            pltpu.VMEM((hps, TQ, TQ + LANES), F32),
            pltpu.VMEM((hps, 1, TQ), F32),
            pltpu.VMEM((hps, 1, TQ), F32),
            pltpu.VMEM((hps, MLA_VP, TQ), F32),
        ],
    )
    return pl.pallas_call(
        _attn_kernel,
        grid_spec=grid_spec,
        out_shape=jax.ShapeDtypeStruct((b, s, nh * MLA_V), BF16),
        compiler_params=pltpu.CompilerParams(
            dimension_semantics=("arbitrary",) * 3, vmem_limit_bytes=VMEM_LIMIT),
        name="attn",
    )(qi, kj, qt, k, vt)


def _mixmem_kernel(x_ref, ug_ref, halo_ref, attn_ref, kmem_ref, vmem_ref,
                   wdw_ref, bdw_ref, lng_ref, lnb_ref, wout_ref,
                   gx_ref, wq_ref, qg_ref, wo_ref, o_ref, ubuf_ref, y_ref):
    tm = TM_MIX
    n_slabs = CONV_CH // LANES
    for c in range(n_slabs):
        sl = slice(c * LANES, (c + 1) * LANES)
        ubuf_ref[c, 0:HALO] = jnp.where(pl.program_id(1) > 0, halo_ref[0, :, sl], 0.0)
        ubuf_ref[c, HALO:] = ug_ref[0, :, sl]
    first = HALO - (CONV_WIDTH - 1)
    for c in range(n_slabs):
        sl = slice(c * LANES, (c + 1) * LANES)
        for parity in range(2):
            for blk in range(tm // (2 * CONV_ROWS)):
                row0 = parity + 2 * blk * CONV_ROWS
                acc = jnp.zeros((CONV_ROWS, LANES), F32) + bdw_ref[:, sl]
                for t in range(CONV_WIDTH):
                    acc = acc + wdw_ref[t:t + 1, sl] * ubuf_ref[c, pl.ds(first + t + row0, CONV_ROWS, stride=2), :]
                y_ref[c, pl.ds(row0, CONV_ROWS, stride=2), :] = acc
    y = jnp.concatenate([y_ref[c] for c in range(n_slabs)], axis=1)
    mu = jnp.mean(y, axis=-1, keepdims=True)
    yc = y - mu
    yn = yc * lax.rsqrt(jnp.mean(yc * yc, axis=-1, keepdims=True) + EPS) * lng_ref[...] + lnb_ref[...]
    u = (yn * jax.nn.sigmoid(yn)).astype(BF16)
    mix = (jnp.dot(u, wout_ref[:CONV_CH, :], preferred_element_type=F32)
           + jnp.dot(attn_ref[0], wout_ref[CONV_CH:, :], preferred_element_type=F32))
    x1 = x_ref[0] + mix

    hq = _rms(x1, gx_ref[...]).astype(BF16)
    q = jnp.dot(hq, wq_ref[...], preferred_element_type=F32)
    outs = []
    for h in range(MEM_HEADS):
        sl = slice(h * MEM_HEAD_DIM, (h + 1) * MEM_HEAD_DIM)
        qn = (_rms(q[:, sl], qg_ref[...]) * (LOG2E / math.sqrt(MEM_HEAD_DIM))).astype(BF16)
        s = lax.dot_general(qn, kmem_ref[0, :, sl], (((1,), (1,)), ((), ())),
                            preferred_element_type=F32)
        p = jnp.exp2(s - jnp.max(s, axis=-1, keepdims=True))
        denom = jnp.sum(p, axis=-1, keepdims=True)
        o = jnp.dot(p.astype(BF16), vmem_ref[0, :, sl], preferred_element_type=F32)
        outs.append((o / denom).astype(BF16))
    o_cat = jnp.concatenate(outs, axis=-1)
    o_ref[0] = x1 + jnp.dot(o_cat, wo_ref[...], preferred_element_type=F32)


def _mixmem(x, ug, attn, kmem, vmem, wdw, bdw, lng, lnb, wout, gx, wq, qg, wo):
    b, s, _ = x.shape
    tm = TM_MIX
    tile = lambda w: pl.BlockSpec((1, tm, w), lambda bi, i: (bi, i, 0))
    halo = pl.BlockSpec((1, HALO, CONV_CH),
                        lambda bi, i: (bi, jnp.maximum(i * (tm // HALO) - 1, 0), 0))
    mem = pl.BlockSpec((1, MEM_LEN, D_MODEL), lambda bi, i: (bi, 0, 0))
    consts = [wdw, bdw, lng, lnb, wout, gx, wq, qg, wo]
    return pl.pallas_call(
        _mixmem_kernel,
        grid=(b, s // tm),
        in_specs=[tile(D_MODEL), tile(CONV_CH), halo, tile(MLA_HEADS * MLA_V), mem, mem]
                 + [_const_spec(c.shape) for c in consts],
        out_specs=tile(D_MODEL),
        out_shape=jax.ShapeDtypeStruct((b, s, D_MODEL), F32),
        scratch_shapes=[pltpu.VMEM((CONV_CH // LANES, HALO + tm, LANES), F32),
                        pltpu.VMEM((CONV_CH // LANES, tm, LANES), F32)],
        compiler_params=pltpu.CompilerParams(
            dimension_semantics=("arbitrary", "arbitrary"), vmem_limit_bytes=VMEM_LIMIT),
        name="mixmem",
    )(x, ug, ug, attn, kmem, vmem, *consts)


def _ffn_kernel(x_ref, g_ref, wup_ref, dwg_ref, dwv_ref, bg_ref, bv_ref, wd_ref,
                o_ref, h_ref, prev_ref, ua_ref, ub_ref, uc_ref, acc_ref):
    tm = TM_FFN

    @pl.when(pl.program_id(1) == 0)
    def _start_of_sequence():
        prev_ref[...] = jnp.zeros(prev_ref.shape, F32)

    x = x_ref[0]
    h_ref[...] = _rms(x, g_ref[...]).astype(BF16)
    acc_ref[...] = x

    def up_proj(c, buf_ref):
        for half in range(2):
            cols = pl.ds(pl.multiple_of(c * FF_CHUNK + half * D_FF, FF_CHUNK), FF_CHUNK)
            up = jnp.dot(h_ref[...], wup_ref[:, cols], preferred_element_type=F32)
            buf_ref[half, 0:FFN_HALO] = prev_ref[half, c]
            buf_ref[half, FFN_HALO:] = up
            prev_ref[half, c] = up[tm - FFN_HALO:]

    def conv_act_down(c, buf_ref):
        def conv(half, dw_ref, b_ref):
            dw = dw_ref[c]
            return (dw[0:1] * buf_ref[half, pl.ds(FFN_HALO - 2, tm), :]
                    + dw[1:2] * buf_ref[half, pl.ds(FFN_HALO - 1, tm), :]
                    + dw[2:3] * buf_ref[half, pl.ds(FFN_HALO, tm), :] + b_ref[c])
        gate = conv(0, dwg_ref, bg_ref)
        val = conv(1, dwv_ref, bv_ref)
        act = (gate * jax.nn.sigmoid(gate) * val).astype(BF16)
        acc_ref[...] += jnp.dot(act, wd_ref[c], preferred_element_type=F32)

    bufs = (ua_ref, ub_ref, uc_ref)
    n_rot = (N_FF_CHUNKS - 2) // 3
    assert N_FF_CHUNKS == 3 * n_rot + 2
    up_proj(0, bufs[0])

    def rotation(t, carry):
        c = 3 * t
        for r in range(3):
            up_proj(c + r + 1, bufs[(r + 1) % 3])
            conv_act_down(c + r, bufs[r])
        return carry

    lax.fori_loop(0, n_rot, rotation, 0)
    up_proj(N_FF_CHUNKS - 1, bufs[1])
    conv_act_down(N_FF_CHUNKS - 2, bufs[0])
    conv_act_down(N_FF_CHUNKS - 1, bufs[1])
    o_ref[0] = acc_ref[...]


def _ffn(x, g, wup, dwg, dwv, bg, bv, wd):
    b, s, _ = x.shape
    tm = TM_FFN
    tile = pl.BlockSpec((1, tm, D_MODEL), lambda bi, i: (bi, i, 0))
    consts = [g, wup, dwg, dwv, bg, bv, wd]
    return pl.pallas_call(
        _ffn_kernel,
        grid=(b, s // tm),
        in_specs=[tile] + [_const_spec(c.shape) for c in consts],
        out_specs=tile,
        out_shape=jax.ShapeDtypeStruct((b, s, D_MODEL), F32),
        scratch_shapes=[
            pltpu.VMEM((tm, D_MODEL), BF16),
            pltpu.VMEM((2, N_FF_CHUNKS, FFN_HALO, FF_CHUNK), F32),
            pltpu.VMEM((2, FFN_HALO + tm, FF_CHUNK), F32),
            pltpu.VMEM((2, FFN_HALO + tm, FF_CHUNK), F32),
            pltpu.VMEM((2, FFN_HALO + tm, FF_CHUNK), F32),
            pltpu.VMEM((tm, D_MODEL), F32),
        ],
        compiler_params=pltpu.CompilerParams(
            dimension_semantics=("arbitrary", "arbitrary"), vmem_limit_bytes=VMEM_LIMIT),
        name="ffn",
    )(x, *consts)


def _row(v):
    return v.reshape(1, -1).astype(F32)


def _col(v, width):
    return jnp.broadcast_to(v.astype(F32)[:, None], (v.shape[0], width))


def _layer(x, kmem, vmem, pos_row, freq, mix_norm_g, w_in, b_conv_in, w_conv_dw, b_conv_dw, conv_ln_g,
           conv_ln_b, q_lat_norm_g, w_uq, kv_lat_norm_g, w_ukv, q_norm_g, k_norm_g, w_out,
           mem_norm_x_g, w_mem_q, mem_q_norm_g, w_mem_o, ffn_norm_g, w_up, w_ffn_dw, b_ffn_dw, w_down):
    s1 = 2 * CONV_CH
    s3 = s1 + MLA_Q_RANK + MLA_KV_RANK
    w_conv = w_in[:, :s1].astype(BF16)
    w_small = jnp.pad(w_in[:, s1:], ((0, 0), (0, 4 * LANES - (IN_COLS - s1)))).astype(BF16)
    wuq_t = w_uq.T.astype(BF16)
    w_ukv3 = w_ukv.reshape(MLA_KV_RANK, MLA_HEADS, MLA_NOPE + MLA_V)
    wuk_t = w_ukv3[:, :, :MLA_NOPE].reshape(MLA_KV_RANK, -1).T.astype(BF16)
    wuv_t = w_ukv3[:, :, MLA_NOPE:].reshape(MLA_KV_RANK, -1).T.astype(BF16)
    qg = _col(q_norm_g * (LOG2E / math.sqrt(MLA_QK)), TM_IN)
    kg = _col(k_norm_g, TM_IN)

    ug, qt, k, vt = _inproj(x, pos_row, _row(mix_norm_g), w_conv, _row(b_conv_in), w_small,
                            _col(q_lat_norm_g, TM_IN), wuq_t, _col(kv_lat_norm_g, TM_IN), wuk_t, wuv_t,
                            qg, kg, freq)
    attn = _attn(qt, k, vt)

    x = _mixmem(x, ug, attn, kmem, vmem, w_conv_dw.astype(F32), _row(b_conv_dw), _row(conv_ln_g),
                _row(conv_ln_b), w_out.astype(BF16), _row(mem_norm_x_g),
                w_mem_q.astype(BF16), _row(mem_q_norm_g), w_mem_o.astype(BF16))

    return _ffn_sublayer(x, ffn_norm_g, w_up, w_ffn_dw, b_ffn_dw, w_down)


def _ffn_sublayer(x, ffn_norm_g, w_up, w_ffn_dw, b_ffn_dw, w_down):
    def chunks(w):
        return jnp.moveaxis(w.reshape(w.shape[:-1] + (N_FF_CHUNKS, FF_CHUNK)), -2, 0)

    dwg = chunks(w_ffn_dw[:, :D_FF]).astype(F32)
    dwv = chunks(w_ffn_dw[:, D_FF:]).astype(F32)
    bg = chunks(b_ffn_dw[None, :D_FF]).astype(F32)
    bv = chunks(b_ffn_dw[None, D_FF:]).astype(F32)
    wd = w_down.reshape(N_FF_CHUNKS, FF_CHUNK, D_MODEL).astype(BF16)
    return _ffn(x, _row(ffn_norm_g), w_up.astype(BF16), dwg, dwv, bg, bv, wd)


def kernel(x, mem, positions, mix_norm_g, w_in, b_conv_in, w_conv_dw, b_conv_dw, conv_ln_g, conv_ln_b, q_lat_norm_g, w_uq, kv_lat_norm_g, w_ukv, q_norm_g, k_norm_g, w_out, mem_norm_x_g, mem_norm_m_g, w_mem_q, w_mem_kv, mem_q_norm_g, mem_k_norm_g, w_mem_o, ffn_norm_g, w_up, w_ffn_dw, b_ffn_dw, w_down):
    depth = mix_norm_g.shape[0]
    b, s, _ = x.shape
    pos_row = positions.reshape(b, 1, s)
    inv_freq = ROPE_THETA ** (-jnp.arange(0, MLA_ROPE, 2, dtype=F32) / MLA_ROPE)
    freq = _col(inv_freq, TM_IN)
    for l in range(depth):
        kmem, vmem = _memkv(mem, _row(mem_norm_m_g[l]), w_mem_kv[l].astype(BF16), _row(mem_k_norm_g[l]))
        x = _layer(x, kmem, vmem, pos_row, freq, mix_norm_g[l], w_in[l], b_conv_in[l], w_conv_dw[l],
                   b_conv_dw[l], conv_ln_g[l], conv_ln_b[l], q_lat_norm_g[l], w_uq[l], kv_lat_norm_g[l],
                   w_ukv[l], q_norm_g[l], k_norm_g[l], w_out[l], mem_norm_x_g[l], w_mem_q[l],
                   mem_q_norm_g[l], w_mem_o[l], ffn_norm_g[l], w_up[l], w_ffn_dw[l], b_ffn_dw[l], w_down[l])
    return x
```
